```python
import math, functools
import jax, jax.numpy as jnp
from jax import lax
import numpy as np

D_MODEL = 2048
BATCH = 2
SEQ = 8192
DEPTH = 1
DEC_BATCH = 32
DEC_SEQ = 1
PAST_LEN = 16384
PAGE_SIZE = 128

HEAD_DIM = 128
N_META = 16
Q_BLOCK = 128
SB_HEADS = D_MODEL // (2 * HEAD_DIM)
SB_KV_HEADS = SB_HEADS // 2
SB_GROUP = SB_HEADS // SB_KV_HEADS
DIFF_HEADS = D_MODEL // (4 * HEAD_DIM)
DIFF_KV_HEADS = DIFF_HEADS // 2
DIFF_GROUP = DIFF_HEADS // DIFF_KV_HEADS
SB_WIDTH = SB_HEADS * HEAD_DIM
DIFF_WIDTH = DIFF_HEADS * 2 * HEAD_DIM
MIX_WIDTH = SB_WIDTH + DIFF_WIDTH
SB_KV_WIDTH = SB_KV_HEADS * HEAD_DIM
DIFF_KV_WIDTH = DIFF_KV_HEADS * 2 * HEAD_DIM
PROJ_SPLITS = (SB_WIDTH,
               SB_WIDTH + SB_KV_WIDTH,
               SB_WIDTH + 2 * SB_KV_WIDTH,
               SB_WIDTH + 2 * SB_KV_WIDTH + DIFF_WIDTH,
               SB_WIDTH + 2 * SB_KV_WIDTH + DIFF_WIDTH + DIFF_KV_WIDTH)
PROJ_WIDTH = SB_WIDTH + 2 * SB_KV_WIDTH + DIFF_WIDTH + 2 * DIFF_KV_WIDTH
REL_BUCKETS = 32
REL_MAX_EXACT = 16
REL_MAX_DIST = 128
N_GROUPS = 4
EXPERTS_PER_GROUP = 8
N_EXPERTS = N_GROUPS * EXPERTS_PER_GROUP
TOP_K = 2
D_EXPERT = D_MODEL // 2
MOE_BLOCK = 128
LN_EPS = 1e-5
DEEPNORM_ALPHA = (2 * DEPTH) ** 0.25
DEEPNORM_BETA = (8 * DEPTH) ** -0.25

kernel_name = 'hybrid_stickbreak_diffattn_hier_moe_step'


def layer_norm(x, g, b):
    xf = x.astype(jnp.float32)
    mu = jnp.mean(xf, axis=-1, keepdims=True)
    var = jnp.mean(jnp.square(xf - mu), axis=-1, keepdims=True)
    return ((xf - mu) * lax.rsqrt(var + LN_EPS) * g + b).astype(x.dtype)


def rms_norm(x, g):
    xf = x.astype(jnp.float32)
    return (xf * lax.rsqrt(jnp.mean(xf * xf, axis=-1, keepdims=True) + LN_EPS) * g).astype(x.dtype)


def t5_bucket(dist):
    n = jnp.maximum(dist, 0)
    nf = jnp.maximum(n, 1).astype(jnp.float32)
    large = REL_MAX_EXACT + (jnp.log(nf / REL_MAX_EXACT) / math.log(REL_MAX_DIST / REL_MAX_EXACT)
                             * (REL_BUCKETS - REL_MAX_EXACT)).astype(jnp.int32)
    large = jnp.minimum(large, REL_BUCKETS - 1)
    return jnp.where(n < REL_MAX_EXACT, n, large)


def project(h, w_in):
    b, t, _ = h.shape
    proj = jnp.einsum('btd,de->bte', h, w_in)
    q_sb, k_sb, v_sb, q_d, k_d, v_d = jnp.split(proj, PROJ_SPLITS, axis=-1)
    return (q_sb.reshape(b, t, SB_KV_HEADS, SB_GROUP, HEAD_DIM),
            k_sb.reshape(b, t, SB_KV_HEADS, HEAD_DIM),
            v_sb.reshape(b, t, SB_KV_HEADS, HEAD_DIM),
            q_d.reshape(b, t, DIFF_KV_HEADS, DIFF_GROUP, 2, HEAD_DIM),
            k_d.reshape(b, t, DIFF_KV_HEADS, 2 * HEAD_DIM),
            v_d.reshape(b, t, DIFF_KV_HEADS, 2 * HEAD_DIM))


def stick_breaking_attend(q, k, v, q_pos, k_pos):
    z = jnp.einsum('bqhgd,bkhd->bhgqk', q, k, preferred_element_type=jnp.float32) * (HEAD_DIM ** -0.5)
    vis = k_pos[None, :] < q_pos[:, None]
    log_1mb = jnp.where(vis, jax.nn.log_sigmoid(-z), 0.0)
    suffix = lax.cumsum(log_1mb, axis=z.ndim - 1, reverse=True) - log_1mb
    w = jnp.where(vis, jnp.exp(jax.nn.log_sigmoid(z) + suffix), 0.0)
    o = jnp.einsum('bhgqk,bkhd->bqhgd', w.astype(v.dtype), v, preferred_element_type=jnp.float32)
    return o.astype(v.dtype)


def diff_attend(q, k, v, q_pos, k_pos, rel_bias, lam):
    b, l = k.shape[:2]
    tq = q.shape[1]
    kk = k.reshape(b, l, DIFF_KV_HEADS, 2, HEAD_DIM)
    s = jnp.einsum('bqhgcd,bkhcd->bhgcqk', q, kk, preferred_element_type=jnp.float32) * (HEAD_DIM ** -0.5)
    bias = rel_bias[t5_bucket(q_pos[:, None] - k_pos[None, :])]
    bias = jnp.moveaxis(bias, -1, 0).reshape(DIFF_KV_HEADS, DIFF_GROUP, 1, tq, l).astype(jnp.float32)
    vis = k_pos[None, :] <= q_pos[:, None]
    p = jax.nn.softmax(jnp.where(vis, s + bias, -jnp.inf), axis=-1)
    a = p[:, :, :, 0] - lam * p[:, :, :, 1]
    o = jnp.einsum('bhgqk,bkhe->bqhge', a.astype(v.dtype), v, preferred_element_type=jnp.float32)
    return o.astype(v.dtype)


def sweep_prompt(fn, q, k, v):
    b, l = q.shape[:2]
    pos = jnp.arange(l, dtype=jnp.int32)
    o_meta = fn(q[:, :N_META], k[:, :N_META], v[:, :N_META], pos[:N_META], pos[:N_META])
    n_blk = (l - N_META) // Q_BLOCK
    qb = jnp.moveaxis(q[:, N_META:].reshape(b, n_blk, Q_BLOCK, *q.shape[2:]), 1, 0)
    pb = pos[N_META:].reshape(n_blk, Q_BLOCK)
    ob = lax.map(lambda a: fn(a[0], k, v, a[1], pos), (qb, pb))
    rest = ob.shape[3:]
    ob = jnp.moveaxis(ob, 0, 1).reshape(b, n_blk * Q_BLOCK, *rest)
    return jnp.concatenate([o_meta, ob], axis=1)


def diff_lambda_value(lam_params, lam_init):
    lp = lam_params.astype(jnp.float32)
    return jnp.exp(jnp.sum(lp[0] * lp[1])) - jnp.exp(jnp.sum(lp[2] * lp[3])) + lam_init


def merge_heads(o_sb, o_d, sb_norm_g, diff_subln_g, lam_init, w_out):
    b, t = o_sb.shape[:2]
    o_sb = rms_norm(o_sb, sb_norm_g).reshape(b, t, SB_WIDTH)
    o_d = (rms_norm(o_d, diff_subln_g) * (1.0 - lam_init)).reshape(b, t, DIFF_WIDTH)
    return jnp.einsum('bte,ed->btd', jnp.concatenate([o_sb, o_d], axis=-1), w_out)


def gather_pages(cache, page_table):
    g = cache[page_table]
    return g.reshape(g.shape[0], g.shape[1] * g.shape[2], *g.shape[3:])


def routed_experts(x, expert_idx, gates, w_gate_up, w_down):
    n_tok, d = x.shape
    n_assign = n_tok * TOP_K
    flat_e = expert_idx.reshape(-1)
    order = jnp.argsort(flat_e)
    sorted_e = flat_e[order]
    counts = jnp.bincount(flat_e, length=N_EXPERTS)
    padded = (counts + MOE_BLOCK - 1) // MOE_BLOCK * MOE_BLOCK
    pad_end = jnp.cumsum(padded)
    pad_start = pad_end - padded
    start = jnp.cumsum(counts) - counts
    slot = pad_start[sorted_e] + jnp.arange(n_assign) - start[sorted_e]
    n_blocks = -(-n_assign // MOE_BLOCK) + N_EXPERTS
    n_slots = n_blocks * MOE_BLOCK
    slot_tok = jnp.full((n_slots,), n_tok, jnp.int32).at[slot].set((order // TOP_K).astype(jnp.int32))
    slot_gate = jnp.zeros((n_slots,), jnp.float32).at[slot].set(gates.reshape(-1)[order])
    blk_exp = jnp.minimum(jnp.searchsorted(pad_end, jnp.arange(n_blocks) * MOE_BLOCK, side='right'), N_EXPERTS - 1)
    x_pad = jnp.concatenate([x, jnp.zeros((1, d), x.dtype)], axis=0)
    xb = x_pad[slot_tok].reshape(n_blocks, MOE_BLOCK, d)

    def expert_mlp(args):
        xs, e = args
        g, u = jnp.split(xs @ w_gate_up[e], 2, axis=-1)
        return (jax.nn.silu(g) * u) @ w_down[e]

    yb = lax.map(expert_mlp, (xb, blk_exp)).reshape(n_slots, d)
    y = jnp.zeros((n_tok + 1, d), jnp.float32).at[slot_tok].add(yb.astype(jnp.float32) * slot_gate[:, None])
    return y[:n_tok].astype(x.dtype)


def hier_moe(h, w_group, b_group, w_expert_router, b_expert_router, w_gate_up, w_down):
    b, t, d = h.shape
    x = h.reshape(b * t, d)
    g_logits = jnp.einsum('nd,dg->ng', x, w_group, preferred_element_type=jnp.float32) + b_group
    p_group = jax.nn.softmax(g_logits, axis=-1)
    grp = jnp.argmax(g_logits, axis=-1)
    p_sel = jnp.take_along_axis(p_group, grp[:, None], axis=-1)
    e_logits = (jnp.einsum('nd,de->ne', x, w_expert_router, preferred_element_type=jnp.float32)
                + b_expert_router).reshape(b * t, N_GROUPS, EXPERTS_PER_GROUP)
    e_logits = jnp.take_along_axis(e_logits, grp[:, None, None], axis=1)[:, 0]
    top_v, top_i = lax.top_k(e_logits, TOP_K)
    gates = p_sel * jax.nn.softmax(top_v, axis=-1)
    expert_idx = (grp[:, None] * EXPERTS_PER_GROUP + top_i).astype(jnp.int32)
    return routed_experts(x, expert_idx, gates, w_gate_up, w_down).reshape(b, t, d)


def setup_inputs(seed: int = 0) -> dict:
    key = jax.random.key(seed)
    ks = jax.random.split(key, 32)
    f32 = jnp.float32
    n_pages = PAST_LEN // PAGE_SIZE
    n_used = DEC_BATCH * n_pages
    n_pool = n_used + max(1, n_used // 4)

    def nrm(k, shape, scale):
        return jax.random.normal(k, shape, f32) * scale

    page_table = jax.random.permutation(ks[6], n_pool)[:n_used].reshape(DEC_BATCH, n_pages).astype(jnp.int32)
    return {
        'x_prompt': nrm(ks[0], (BATCH, SEQ, D_MODEL), 1.0),
        'x_sample': nrm(ks[1], (DEC_BATCH, DEC_SEQ, D_MODEL), 1.0),
        'cache_k_sb': nrm(ks[2], (DEPTH, n_pool, PAGE_SIZE, SB_KV_HEADS, HEAD_DIM), 1.0),
        'cache_v_sb': nrm(ks[3], (DEPTH, n_pool, PAGE_SIZE, SB_KV_HEADS, HEAD_DIM), 1.0),
        'cache_k_diff': nrm(ks[4], (DEPTH, n_pool, PAGE_SIZE, DIFF_KV_HEADS, 2 * HEAD_DIM), 1.0),
        'cache_v_diff': nrm(ks[5], (DEPTH, n_pool, PAGE_SIZE, DIFF_KV_HEADS, 2 * HEAD_DIM), 1.0),
        'page_table': page_table,
        'meta_tokens': nrm(ks[7], (N_META, D_MODEL), 1.0),
        'ln_in_g': 1.0 + nrm(ks[8], (D_MODEL,), 0.02),
        'ln_in_b': nrm(ks[9], (D_MODEL,), 0.02),
        'rel_bias': nrm(ks[10], (REL_BUCKETS, DIFF_HEADS), 0.1),
        'w_in': nrm(ks[11], (DEPTH, D_MODEL, PROJ_WIDTH), D_MODEL ** -0.5),
        'sb_norm_g': 1.0 + nrm(ks[12], (DEPTH, HEAD_DIM), 0.02),
        'diff_lambda': nrm(ks[13], (DEPTH, 4, HEAD_DIM), 0.1),
        'diff_subln_g': 1.0 + nrm(ks[14], (DEPTH, 2 * HEAD_DIM), 0.02),
        'w_out': nrm(ks[15], (DEPTH, MIX_WIDTH, D_MODEL), MIX_WIDTH ** -0.5 * DEEPNORM_BETA),
        'ln1_g': 1.0 + nrm(ks[16], (DEPTH, D_MODEL), 0.02),
        'ln1_b': nrm(ks[17], (DEPTH, D_MODEL), 0.02),
        'w_group': nrm(ks[18], (DEPTH, D_MODEL, N_GROUPS), D_MODEL ** -0.5),
        'b_group': nrm(ks[19], (DEPTH, N_GROUPS), 0.01),
        'w_expert_router': nrm(ks[20], (DEPTH, D_MODEL, N_EXPERTS), D_MODEL ** -0.5),
        'b_expert_router': nrm(ks[21], (DEPTH, N_EXPERTS), 0.01),
        'w_gate_up': nrm(ks[22], (DEPTH, N_EXPERTS, D_MODEL, 2 * D_EXPERT), D_MODEL ** -0.5),
        'w_down': nrm(ks[23], (DEPTH, N_EXPERTS, D_EXPERT, D_MODEL), D_EXPERT ** -0.5 * DEEPNORM_BETA),
        'ln2_g': 1.0 + nrm(ks[24], (DEPTH, D_MODEL), 0.02),
        'ln2_b': nrm(ks[25], (DEPTH, D_MODEL), 0.02),
    }


def reference(x_prompt, x_sample, cache_k_sb, cache_v_sb, cache_k_diff, cache_v_diff, page_table,
              meta_tokens, ln_in_g, ln_in_b, rel_bias, w_in, sb_norm_g, diff_lambda, diff_subln_g,
              w_out, ln1_g, ln1_b, w_group, b_group, w_expert_router, b_expert_router,
              w_gate_up, w_down, ln2_g, ln2_b):
    b = x_prompt.shape[0]
    t_dec = x_sample.shape[1]
    past = page_table.shape[1] * cache_k_sb.shape[2]
    meta = jnp.broadcast_to(meta_tokens.astype(x_prompt.dtype)[None], (b, N_META, x_prompt.shape[-1]))
    hp = layer_norm(jnp.concatenate([meta, x_prompt], axis=1), ln_in_g, ln_in_b)
    hs = layer_norm(x_sample, ln_in_g, ln_in_b)
    k_pos_s = jnp.arange(past + t_dec, dtype=jnp.int32)
    q_pos_s = past + jnp.arange(t_dec, dtype=jnp.int32)
    kp_sb, vp_sb, kp_d, vp_d = [], [], [], []
    ks_sb, vs_sb, ks_d, vs_d = [], [], [], []
    for li in range(DEPTH):
        lam_init = 0.8 - 0.6 * math.exp(-0.3 * li)
        lam = diff_lambda_value(diff_lambda[li], lam_init)
        diff_fn = functools.partial(diff_attend, rel_bias=rel_bias, lam=lam)

        q_sb, k_sb, v_sb, q_d, k_d, v_d = project(hp, w_in[li])
        o_sb = sweep_prompt(stick_breaking_attend, q_sb, k_sb, v_sb)
        o_d = sweep_prompt(diff_fn, q_d, k_d, v_d)
        att = merge_heads(o_sb, o_d, sb_norm_g[li], diff_subln_g[li], lam_init, w_out[li])
        hp = layer_norm(DEEPNORM_ALPHA * hp + att, ln1_g[li], ln1_b[li])
        ffn = hier_moe(hp, w_group[li], b_group[li], w_expert_router[li], b_expert_router[li], w_gate_up[li], w_down[li])
        hp = layer_norm(DEEPNORM_ALPHA * hp + ffn, ln2_g[li], ln2_b[li])
        kp_sb.append(k_sb); vp_sb.append(v_sb); kp_d.append(k_d); vp_d.append(v_d)

        qn_sb, kn_sb, vn_sb, qn_d, kn_d, vn_d = project(hs, w_in[li])
        k_all_sb = jnp.concatenate([gather_pages(cache_k_sb[li], page_table), kn_sb.astype(cache_k_sb.dtype)], axis=1)
        v_all_sb = jnp.concatenate([gather_pages(cache_v_sb[li], page_table), vn_sb.astype(cache_v_sb.dtype)], axis=1)
        k_all_d = jnp.concatenate([gather_pages(cache_k_diff[li], page_table), kn_d.astype(cache_k_diff.dtype)], axis=1)
        v_all_d = jnp.concatenate([gather_pages(cache_v_diff[li], page_table), vn_d.astype(cache_v_diff.dtype)], axis=1)
        os_sb = stick_breaking_attend(qn_sb, k_all_sb, v_all_sb, q_pos_s, k_pos_s)
        os_d = diff_fn(qn_d, k_all_d, v_all_d, q_pos_s, k_pos_s)
        att_s = merge_heads(os_sb, os_d, sb_norm_g[li], diff_subln_g[li], lam_init, w_out[li])
        hs = layer_norm(DEEPNORM_ALPHA * hs + att_s, ln1_g[li], ln1_b[li])
        ffn_s = hier_moe(hs, w_group[li], b_group[li], w_expert_router[li], b_expert_router[li], w_gate_up[li], w_down[li])
        hs = layer_norm(DEEPNORM_ALPHA * hs + ffn_s, ln2_g[li], ln2_b[li])
        ks_sb.append(kn_sb); vs_sb.append(vn_sb); ks_d.append(kn_d); vs_d.append(vn_d)

    y_prompt = hp[:, N_META:]
    return (y_prompt, hs,
            jnp.stack(kp_sb), jnp.stack(vp_sb), jnp.stack(kp_d), jnp.stack(vp_d),
            jnp.stack(ks_sb), jnp.stack(vs_sb), jnp.stack(ks_d), jnp.stack(vs_d))
```

```python
import functools
import math

import jax
import jax.numpy as jnp
from jax import lax
from jax.experimental import pallas as pl
from jax.experimental.pallas import tpu as pltpu

F32 = jnp.float32
BF16 = jnp.bfloat16
I32 = jnp.int32

LANES = 128
MXU_DIM = 256
LN_EPS = 1e-5
REL_BUCKETS, REL_MAX_EXACT, REL_MAX_DIST = 32, 16, 128
TOP_K = 2
LOG2E = math.log2(math.e)
MASKED = -1e30
SB_UNDERFLOW = -104.0
ROW_TILE = 256
PROJ_TILE = 320
DEC_PAGES = 8
FAR_TILES = 4
VMEM_LIMIT = 56 * 1024 * 1024

HIGHEST = lax.Precision.HIGHEST
NT_DIMS = (((1,), (1,)), ((), ()))


def _params(sem, vmem=VMEM_LIMIT):
    return pltpu.CompilerParams(dimension_semantics=sem, vmem_limit_bytes=vmem)


def _layer_norm(x, g, b):
    mu = jnp.mean(x, axis=-1, keepdims=True)
    xc = x - mu
    var = jnp.mean(xc * xc, axis=-1, keepdims=True)
    return xc * lax.rsqrt(var + LN_EPS) * g + b


def _rms_norm(x, g):
    return x * lax.rsqrt(jnp.mean(x * x, axis=-1, keepdims=True) + LN_EPS) * g


def _split_bf16(x):
    hi = x.astype(BF16)
    lo = (x - hi.astype(F32)).astype(BF16)
    return hi, lo


def _lambda_value(lp, lam_init):
    a = jnp.sum(lp[0:1] * lp[1:2], axis=-1, keepdims=True)
    b = jnp.sum(lp[2:3] * lp[3:4], axis=-1, keepdims=True)
    return jnp.exp(a) - jnp.exp(b) + lam_init


def _ln_proj_prompt_kernel(x_ref, g_ref, b_ref, w_ref, h_ref, qsb_ref, qd_ref,
                           ksb16_ref, vsb16_ref, kd16_ref, vd16_ref,
                           ksb_ref, vsb_ref, kd_ref, vd_ref, *, splits):
    h = _layer_norm(x_ref[0], g_ref[...], b_ref[...])
    h_ref[0] = h
    hb = h.astype(BF16)

    def proj(seg):
        lo, hi = splits[seg], splits[seg + 1]
        return jnp.dot(hb, w_ref[:, lo:hi], preferred_element_type=F32)

    qsb_ref[0] = proj(0).astype(BF16)
    for seg, full_ref, half_ref in ((1, ksb_ref, ksb16_ref), (2, vsb_ref, vsb16_ref)):
        y = proj(seg)
        full_ref[0] = y
        half_ref[0] = y.astype(BF16)
    qd_ref[0] = proj(3).astype(BF16)
    for seg, full_ref, half_ref in ((4, kd_ref, kd16_ref), (5, vd_ref, vd16_ref)):
        y = proj(seg)
        full_ref[0] = y
        half_ref[0] = y.astype(BF16)


def _ln_proj_prompt(xpad, g, b, w16, splits, seq_len):
    bsz, lp, d = xpad.shape
    widths = [splits[i + 1] - splits[i] for i in range(6)]
    tm = PROJ_TILE
    assert lp % tm == 0
    row = lambda wd: pl.BlockSpec((1, tm, wd), lambda bi, i: (bi, i, 0))
    const = lambda shp: pl.BlockSpec(shp, lambda bi, i: (0,) * len(shp))
    out_shape = (
        jax.ShapeDtypeStruct((bsz, lp, d), F32),
        jax.ShapeDtypeStruct((bsz, lp, widths[0]), BF16),
        jax.ShapeDtypeStruct((bsz, lp, widths[3]), BF16),
        jax.ShapeDtypeStruct((bsz, lp, widths[1]), BF16),
        jax.ShapeDtypeStruct((bsz, lp, widths[2]), BF16),
        jax.ShapeDtypeStruct((bsz, lp, widths[4]), BF16),
        jax.ShapeDtypeStruct((bsz, lp, widths[5]), BF16),
        jax.ShapeDtypeStruct((bsz, seq_len, widths[1]), F32),
        jax.ShapeDtypeStruct((bsz, seq_len, widths[2]), F32),
        jax.ShapeDtypeStruct((bsz, seq_len, widths[4]), F32),
        jax.ShapeDtypeStruct((bsz, seq_len, widths[5]), F32),
    )
    out_specs = (row(d), row(widths[0]), row(widths[3]), row(widths[1]), row(widths[2]),
                 row(widths[4]), row(widths[5]), row(widths[1]), row(widths[2]), row(widths[4]),
                 row(widths[5]))
    return pl.pallas_call(
        functools.partial(_ln_proj_prompt_kernel, splits=tuple(splits)),
        grid=(bsz, lp // tm),
        in_specs=[row(d), const((1, d)), const((1, d)),
                  pl.BlockSpec(w16.shape, lambda bi, i: (0, 0), pipeline_mode=pl.Buffered(1))],
        out_specs=out_specs,
        out_shape=out_shape,
        compiler_params=_params(("arbitrary", "arbitrary")),
    )(xpad, g, b, w16)


def _ln_proj_rows_kernel(x_ref, g_ref, b_ref, w_ref, h_ref, p_ref):
    h = _layer_norm(x_ref[...], g_ref[...], b_ref[...])
    h_ref[...] = h
    p_ref[...] = jnp.dot(h, w_ref[...], precision=HIGHEST, preferred_element_type=F32)


def _ln_proj_rows(x, g, b, w):
    n, d = x.shape
    pw = w.shape[1]
    tn = 512
    assert pw % tn == 0
    return pl.pallas_call(
        _ln_proj_rows_kernel,
        grid=(pw // tn,),
        in_specs=[pl.BlockSpec((n, d), lambda j: (0, 0)), pl.BlockSpec((1, d), lambda j: (0, 0)),
                  pl.BlockSpec((1, d), lambda j: (0, 0)), pl.BlockSpec((d, tn), lambda j: (0, j))],
        out_specs=(pl.BlockSpec((n, d), lambda j: (0, 0)), pl.BlockSpec((n, tn), lambda j: (0, j))),
        out_shape=(jax.ShapeDtypeStruct((n, d), F32), jax.ShapeDtypeStruct((n, pw), F32)),
        compiler_params=_params(("arbitrary",)),
    )(x, g, b, w)


def _bias_from_dist(dist, rel_ref, head):
    n = jnp.maximum(dist, 0)
    nf = jnp.maximum(n, 1).astype(F32)
    large = REL_MAX_EXACT + (jnp.log(nf / REL_MAX_EXACT) / math.log(REL_MAX_DIST / REL_MAX_EXACT)
                             * (REL_BUCKETS - REL_MAX_EXACT)).astype(I32)
    large = jnp.minimum(large, REL_BUCKETS - 1)
    bucket = jnp.where(n < REL_MAX_EXACT, n, large)
    out = jnp.zeros(dist.shape, F32)
    for bkt in range(REL_BUCKETS):
        out = jnp.where(bucket == bkt, rel_ref[bkt, head], out)
    return out * LOG2E


def _bias_tables_kernel(rel_ref, near_ref, far_ref, dec_ref, *, n_heads, past):
    i = lax.broadcasted_iota(I32, (LANES, 2 * LANES), 0)
    j = lax.broadcasted_iota(I32, (LANES, 2 * LANES), 1)
    dist = i + LANES - j
    kpos = lax.broadcasted_iota(I32, dec_ref.shape[1:], 1)
    for h in range(n_heads):
        near_ref[h] = jnp.where(dist >= 0, _bias_from_dist(dist, rel_ref, h), MASKED)
        far_ref[h] = jnp.full(far_ref.shape[1:], rel_ref[REL_BUCKETS - 1, h] * LOG2E, F32)
        dec_ref[h] = _bias_from_dist(past - kpos, rel_ref, h)


def _bias_tables(rel_bias, past):
    n_heads = rel_bias.shape[1]
    return pl.pallas_call(
        functools.partial(_bias_tables_kernel, n_heads=n_heads, past=past),
        in_specs=[pl.BlockSpec(memory_space=pltpu.SMEM)],
        out_shape=(jax.ShapeDtypeStruct((n_heads, LANES, 2 * LANES), F32),
                   jax.ShapeDtypeStruct((n_heads, LANES, LANES), F32),
                   jax.ShapeDtypeStruct((n_heads, 8, past), F32)),
    )(rel_bias)


def _sb_tile(z, vis, tri, carry):
    l = -(jnp.maximum(z, 0.0) + jnp.log(1.0 + jnp.exp(-jnp.abs(z))))
    if vis is not None:
        l = jnp.where(vis, l, 0.0)
    hi, lo = _split_bf16(l)
    t2 = jnp.dot(jnp.concatenate([hi, lo], axis=1), tri, preferred_element_type=F32)
    nt = z.shape[1]
    incl, total = t2[:, :nt], t2[:, nt:]
    w = jnp.exp(z + incl + carry)
    if vis is not None:
        w = jnp.where(vis, w, 0.0)
    return w, carry + total


def _sb_prompt_kernel(q_ref, k_ref, v_ref, tri_ref, g_ref, o_ref, acc_ref, carry_ref, *, scale):
    qi = pl.program_id(2)
    q = q_ref[0]
    group = q.shape[1] // LANES
    q2 = jnp.concatenate([q[:, g * LANES:(g + 1) * LANES] for g in range(group)], axis=0)
    m = group * LANES
    acc_ref[...] = jnp.zeros_like(acc_ref)
    carry_ref[...] = jnp.zeros_like(carry_ref)
    rowpos = lax.broadcasted_iota(I32, (m, LANES), 0) & (LANES - 1)
    col = lax.broadcasted_iota(I32, (m, LANES), 1)
    tri = tri_ref[...]

    def cond(state):
        j, live = state
        return jnp.logical_and(j >= 0, live > SB_UNDERFLOW)

    def body(state):
        j, _ = state
        ks = pl.multiple_of(j * LANES, LANES)
        k = k_ref[0, pl.ds(ks, LANES), :]
        v = v_ref[0, pl.ds(ks, LANES), :]
        z = lax.dot_general(q2, k, NT_DIMS, preferred_element_type=F32) * scale
        vis = col < rowpos + (qi - j) * LANES
        w, carry = _sb_tile(z, vis, tri, carry_ref[...])
        acc_ref[...] += jnp.dot(w.astype(BF16), v, preferred_element_type=F32)
        carry_ref[...] = carry
        return j - 1, jnp.max(carry)

    lax.while_loop(cond, body, (qi, jnp.float32(0.0)))
    o = _rms_norm(acc_ref[...], g_ref[...]).astype(o_ref.dtype)
    o_ref[0] = jnp.concatenate([o[g * LANES:(g + 1) * LANES] for g in range(group)], axis=1)


def _sb_prompt(q16, k16, v16, tri, g, n_kv):
    bsz, lp, width = q16.shape
    gw = width // n_kv
    hd = k16.shape[2] // n_kv
    nq = lp // LANES
    return pl.pallas_call(
        functools.partial(_sb_prompt_kernel, scale=hd ** -0.5),
        grid=(bsz, n_kv, nq),
        in_specs=[pl.BlockSpec((1, LANES, gw), lambda b, h, i: (b, i, h)),
                  pl.BlockSpec((1, lp, hd), lambda b, h, i: (b, 0, h)),
                  pl.BlockSpec((1, lp, hd), lambda b, h, i: (b, 0, h)),
                  pl.BlockSpec(tri.shape, lambda b, h, i: (0, 0)),
                  pl.BlockSpec((1, hd), lambda b, h, i: (0, 0))],
        out_specs=pl.BlockSpec((1, LANES, gw), lambda b, h, i: (b, i, h)),
        out_shape=jax.ShapeDtypeStruct((bsz, lp, width), BF16),
        scratch_shapes=[pltpu.VMEM((gw, hd), F32), pltpu.VMEM((gw, LANES), F32)],
        compiler_params=_params(("arbitrary",) * 3),
    )(q16, k16, v16, tri, g)


def _sb_decode_kernel(pt_ref, q_ref, tri_ref, g_ref, k_hbm, v_hbm, o_ref, kbuf, vbuf, sem, *,
                      scale, n_kv, n_pages, layer):
    s = pl.program_id(0)
    q = q_ref[0]
    hd = g_ref.shape[1]
    n_heads = q.shape[1] // hd
    group = n_heads // n_kv
    zeros = jnp.zeros((1, hd), F32)
    rows = []
    for n in range(n_heads):
        seg = q[:, n * hd:(n + 1) * hd]
        rows.append(jnp.concatenate([seg if h == n // group else zeros for h in range(n_kv)], axis=1))
    qb = jnp.concatenate(rows, axis=0)
    q_hi, q_lo = _split_bf16(qb)
    tri = tri_ref[...]

    def page_copy(j, slot, buf, hbm, which):
        page = pt_ref[s * n_pages + j]
        return pltpu.make_async_copy(hbm.at[layer, page], buf.at[slot], sem.at[which, slot])

    def fetch(j, slot):
        page_copy(j, slot, kbuf, k_hbm, 0).start()
        page_copy(j, slot, vbuf, v_hbm, 1).start()

    fetch(n_pages - 1, 0)

    def cond(state):
        j, live, _, _ = state
        return jnp.logical_and(j >= 0, live > SB_UNDERFLOW)

    def body(state):
        j, _, carry, acc = state
        slot = (n_pages - 1 - j) % 2
        page_copy(j, slot, kbuf, k_hbm, 0).wait()
        page_copy(j, slot, vbuf, v_hbm, 1).wait()

        @pl.when(j > 0)
        def _():
            fetch(j - 1, 1 - slot)

        k_hi, k_lo = _split_bf16(kbuf[slot])
        v_hi, v_lo = _split_bf16(vbuf[slot])
        z = (lax.dot_general(q_hi, k_hi, NT_DIMS, preferred_element_type=F32)
             + lax.dot_general(q_lo, k_hi, NT_DIMS, preferred_element_type=F32)
             + lax.dot_general(q_hi, k_lo, NT_DIMS, preferred_element_type=F32)) * scale
        w, carry = _sb_tile(z, None, tri, carry)
        w_hi, w_lo = _split_bf16(w)
        acc = acc + (jnp.dot(w_hi, v_hi, preferred_element_type=F32)
                     + jnp.dot(w_lo, v_hi, preferred_element_type=F32)
                     + jnp.dot(w_hi, v_lo, preferred_element_type=F32))
        return j - 1, jnp.max(carry), carry, acc

    init = (jnp.int32(n_pages - 1), jnp.float32(0.0), jnp.zeros((n_heads, LANES), F32),
            jnp.zeros((n_heads, n_kv * hd), F32))
    j_end, _, _, acc = lax.while_loop(cond, body, init)

    @pl.when(j_end >= 0)
    def _():
        slot = (n_pages - 1 - j_end) % 2
        page_copy(j_end, slot, kbuf, k_hbm, 0).wait()
        page_copy(j_end, slot, vbuf, v_hbm, 1).wait()

    outs = []
    for n in range(n_heads):
        h = n // group
        outs.append(_rms_norm(acc[n:n + 1, h * hd:(h + 1) * hd], g_ref[...]))
    o_ref[0] = jnp.concatenate(outs, axis=1)


def _sb_decode(pt_flat, q, tri, g, cache_k, cache_v, layer):
    db, width = q.shape
    depth, n_pool, ps, n_kv, hd = cache_k.shape
    assert ps == LANES
    n_pages = pt_flat.shape[0] // db
    kc = cache_k.reshape(depth, n_pool, ps, n_kv * hd)
    vc = cache_v.reshape(depth, n_pool, ps, n_kv * hd)
    grid_spec = pltpu.PrefetchScalarGridSpec(
        num_scalar_prefetch=1,
        grid=(db,),
        in_specs=[pl.BlockSpec((1, 1, width), lambda s, pt: (s, 0, 0)),
                  pl.BlockSpec(tri.shape, lambda s, pt: (0, 0)),
                  pl.BlockSpec((1, hd), lambda s, pt: (0, 0)),
                  pl.BlockSpec(memory_space=pl.ANY), pl.BlockSpec(memory_space=pl.ANY)],
        out_specs=pl.BlockSpec((1, 1, width), lambda s, pt: (s, 0, 0)),
        scratch_shapes=[pltpu.VMEM((2, ps, n_kv * hd), F32), pltpu.VMEM((2, ps, n_kv * hd), F32),
                        pltpu.SemaphoreType.DMA((2, 2))],
    )
    out = pl.pallas_call(
        functools.partial(_sb_decode_kernel, scale=hd ** -0.5, n_kv=n_kv, n_pages=n_pages, layer=layer),
        grid_spec=grid_spec,
        out_shape=jax.ShapeDtypeStruct((db, 1, width), F32),
        compiler_params=_params(("arbitrary",)),
    )(pt_flat, q.reshape(db, 1, width), tri, g, kc, vc)
    return out.reshape(db, width)


def _diff_prompt_kernel(q_ref, k_ref, v_ref, near_ref, far_ref, lam_ref, g_ref, o_ref,
                        m_ref, l_ref, acc_ref, *, scale, lam_init):
    qi = pl.program_id(2)
    q = q_ref[0]
    hd = k_ref.shape[2] // 2
    group = q.shape[1] // (2 * hd)
    rows = group * LANES
    qm = [jnp.concatenate([q[:, (2 * g + c) * hd:(2 * g + c + 1) * hd] for g in range(group)], axis=0)
          for c in range(2)]
    c1 = scale * LOG2E

    def scores(c, ks, width):
        k = k_ref[0, pl.ds(ks, width), c * hd:(c + 1) * hd]
        return lax.dot_general(qm[c], k, NT_DIMS, preferred_element_type=F32) * c1

    def near_chunk(ks, bias, width):
        v = v_ref[0, pl.ds(ks, width), :]
        for c in range(2):
            s = scores(c, ks, width) + bias
            mx = jnp.max(s, axis=-1, keepdims=True)
            p = jnp.exp2(s - mx)
            m_ref[c] = jnp.broadcast_to(mx, (rows, LANES))
            l_ref[c] = jnp.broadcast_to(jnp.sum(p, axis=-1, keepdims=True), (rows, LANES))
            acc_ref[c] = jnp.dot(p.astype(BF16), v, preferred_element_type=F32)

    @pl.when(qi == 0)
    def _():
        near_chunk(0, near_ref[0, :, LANES:], LANES)

    @pl.when(qi > 0)
    def _():
        near_chunk(pl.multiple_of((qi - 1) * LANES, LANES), near_ref[0], 2 * LANES)

    def far_chunk(ks, width):
        v = v_ref[0, pl.ds(ks, width), :]
        fb = far_ref[0]
        for c in range(2):
            s = scores(c, ks, width)
            m_old = m_ref[c]
            m_new = jnp.maximum(m_old, jnp.max(s, axis=-1, keepdims=True) + fb)
            l_old = l_ref[c]
            p = jnp.exp2(s - (m_new - fb)[:, :1])
            pv = jnp.dot(p.astype(BF16), v, preferred_element_type=F32)
            alpha = jnp.exp2(m_old - m_new)
            l_ref[c] = alpha * l_old + jnp.sum(p, axis=-1, keepdims=True)
            acc_ref[c] = alpha[:, :1] * acc_ref[c] + pv
            m_ref[c] = m_new

    n_far = jnp.maximum(qi - 1, 0)
    wide = FAR_TILES * LANES

    def wide_body(t, carry):
        far_chunk(pl.multiple_of(t * wide, wide), wide)
        return carry

    lax.fori_loop(0, n_far // FAR_TILES, wide_body, 0)

    def narrow_body(t, carry):
        far_chunk(pl.multiple_of(t * LANES, LANES), LANES)
        return carry

    lax.fori_loop((n_far // FAR_TILES) * FAR_TILES, n_far, narrow_body, 0)

    lam = _lambda_value(lam_ref[...], lam_init)
    a = acc_ref[0] / l_ref[0][:, :1] - lam * (acc_ref[1] / l_ref[1][:, :1])
    o = _rms_norm(a, g_ref[...] * (1.0 - lam_init)).astype(o_ref.dtype)
    o_ref[0] = jnp.concatenate([o[g * LANES:(g + 1) * LANES] for g in range(group)], axis=1)


def _diff_prompt(q16, k16, v16, near, far, lam_params, g, n_kv, lam_init):
    bsz, lp, width = q16.shape
    gw = width // n_kv
    kw = k16.shape[2] // n_kv
    hd = kw // 2
    group = gw // kw
    rows = group * LANES
    nq = lp // LANES
    near_g = near.reshape(n_kv, rows, 2 * LANES)
    far_g = far.reshape(n_kv, rows, LANES)
    return pl.pallas_call(
        functools.partial(_diff_prompt_kernel, scale=hd ** -0.5, lam_init=lam_init),
        grid=(bsz, n_kv, nq),
        in_specs=[pl.BlockSpec((1, LANES, gw), lambda b, h, i: (b, i, h)),
                  pl.BlockSpec((1, lp, kw), lambda b, h, i: (b, 0, h)),
                  pl.BlockSpec((1, lp, kw), lambda b, h, i: (b, 0, h)),
                  pl.BlockSpec((1, rows, 2 * LANES), lambda b, h, i: (h, 0, 0)),
                  pl.BlockSpec((1, rows, LANES), lambda b, h, i: (h, 0, 0)),
                  pl.BlockSpec(lam_params.shape, lambda b, h, i: (0, 0)),
                  pl.BlockSpec((1, kw), lambda b, h, i: (0, 0))],
        out_specs=pl.BlockSpec((1, LANES, gw), lambda b, h, i: (b, i, h)),
        out_shape=jax.ShapeDtypeStruct((bsz, lp, width), BF16),
        scratch_shapes=[pltpu.VMEM((2, rows, LANES), F32), pltpu.VMEM((2, rows, LANES), F32),
                        pltpu.VMEM((2, rows, kw), F32)],
        compiler_params=_params(("arbitrary",) * 3),
    )(q16, k16, v16, near_g, far_g, lam_params, g)


def _diff_decode_kernel(pt_ref, q_ref, kn_ref, vn_ref, bias_ref, bias0_ref, lam_ref, g_ref, *rest,
                        scale, lam_init, n_kv, pages):
    k_refs, v_refs = rest[:pages], rest[pages:2 * pages]
    o_ref, m_ref, l_ref, acc_ref = rest[2 * pages:]
    c = pl.program_id(1)
    q = q_ref[0]
    kw = g_ref.shape[1]
    hd = kw // 2
    n_heads = q.shape[1] // kw
    group = n_heads // n_kv
    n_rows = 2 * n_heads
    zeros = jnp.zeros((1, hd), F32)
    rows = []
    for mp in range(2):
        for hg in range(n_heads):
            seg = q[:, (2 * hg + mp) * hd:(2 * hg + mp + 1) * hd]
            want = (hg // group) * 2 + mp
            rows.append(jnp.concatenate([seg if t == want else zeros for t in range(2 * n_kv)], axis=1))
    qb = jnp.concatenate(rows, axis=0)
    c1 = scale * LOG2E

    @pl.when(c == 0)
    def _():
        s0 = jnp.sum(qb * kn_ref[0], axis=-1, keepdims=True) * c1 + bias0_ref[...]
        m_ref[...] = jnp.broadcast_to(s0, m_ref.shape)
        l_ref[...] = jnp.ones_like(l_ref)
        acc_ref[...] = jnp.broadcast_to(vn_ref[0], acc_ref.shape)

    k = jnp.concatenate([r[...] for r in k_refs], axis=0).astype(BF16)
    v = jnp.concatenate([r[...] for r in v_refs], axis=0).astype(BF16)
    s = lax.dot_general(qb.astype(BF16), k, NT_DIMS, preferred_element_type=F32) * c1 + bias_ref[...]
    m_old = m_ref[...]
    m_new = jnp.maximum(m_old, jnp.max(s, axis=-1, keepdims=True))
    p = jnp.exp2(s - m_new[:, :1])
    alpha = jnp.exp2(m_old - m_new)
    l_ref[...] = alpha * l_ref[...] + jnp.sum(p, axis=-1, keepdims=True)
    acc_ref[...] = alpha[:, :1] * acc_ref[...] + jnp.dot(p.astype(BF16), v, preferred_element_type=F32)
    m_ref[...] = m_new

    @pl.when(c == pl.num_programs(1) - 1)
    def _():
        lam = _lambda_value(lam_ref[...], lam_init)
        o = acc_ref[...] / l_ref[...][:, :1]
        a = o[:n_heads] - lam * o[n_heads:]
        gain = g_ref[...] * (1.0 - lam_init)
        outs = []
        for hg in range(n_heads):
            h = hg // group
            outs.append(_rms_norm(a[hg:hg + 1, h * kw:(h + 1) * kw], gain))
        o_ref[0] = jnp.concatenate(outs, axis=1)


def _diff_decode(pt_flat, q, k_new, v_new, dec_bias, bias0, lam_params, g, cache_k, cache_v, layer,
                 lam_init):
    db, width = q.shape
    depth, n_pool, ps, n_kv, kw = cache_k.shape
    n_pages = pt_flat.shape[0] // db
    pages = DEC_PAGES
    assert n_pages % pages == 0
    n_heads = width // kw
    n_rows = 2 * n_heads
    cw = n_kv * kw
    kc = cache_k.reshape(depth, n_pool, ps, cw)
    vc = cache_v.reshape(depth, n_pool, ps, cw)

    def page_spec(t):
        return pl.BlockSpec((None, None, ps, cw),
                            lambda s, c, pt: (layer, pt[s * n_pages + c * pages + t], 0, 0))

    seq = lambda wd: pl.BlockSpec((1, 1, wd), lambda s, c, pt: (s, 0, 0))
    const = lambda shp: pl.BlockSpec(shp, lambda s, c, pt: (0,) * len(shp))
    grid_spec = pltpu.PrefetchScalarGridSpec(
        num_scalar_prefetch=1,
        grid=(db, n_pages // pages),
        in_specs=[seq(width), seq(cw), seq(cw),
                  pl.BlockSpec((n_rows, pages * ps), lambda s, c, pt: (0, c)),
                  const((n_rows, 1)), const(lam_params.shape), const((1, kw))]
                 + [page_spec(t) for t in range(pages)] * 2,
        out_specs=seq(width),
        scratch_shapes=[pltpu.VMEM((n_rows, LANES), F32), pltpu.VMEM((n_rows, LANES), F32),
                        pltpu.VMEM((n_rows, cw), F32)],
    )
    out = pl.pallas_call(
        functools.partial(_diff_decode_kernel, scale=(kw // 2) ** -0.5, lam_init=lam_init, n_kv=n_kv,
                          pages=pages),
        grid_spec=grid_spec,
        out_shape=jax.ShapeDtypeStruct((db, 1, width), F32),
        compiler_params=_params(("arbitrary", "arbitrary")),
    )(pt_flat, q.reshape(db, 1, width), k_new.reshape(db, 1, cw), v_new.reshape(db, 1, cw), dec_bias,
      bias0, lam_params, g, *([kc] * pages), *([vc] * pages))
    return out.reshape(db, width)


def _route(logits, bias, n_groups, per_group):
    x = logits + bias
    lane = lax.broadcasted_iota(I32, x.shape, 1).astype(F32)
    big = jnp.float32(4 * LANES)
    neg = jnp.float32(-jnp.inf)

    def first_argmax(vals, mask):
        mx = jnp.max(jnp.where(mask, vals, neg), axis=-1, keepdims=True)
        idx = jnp.min(jnp.where(jnp.logical_and(mask, vals == mx), lane, big), axis=-1, keepdims=True)
        return mx, idx

    gmask = lane < n_groups
    gmax, grp = first_argmax(x, gmask)
    p_sel = 1.0 / jnp.sum(jnp.where(gmask, jnp.exp(x - gmax), 0.0), axis=-1, keepdims=True)
    emask = jnp.logical_and(lane >= n_groups + grp * per_group, lane < n_groups + (grp + 1) * per_group)
    v1, i1 = first_argmax(x, emask)
    v2, i2 = first_argmax(x, jnp.logical_and(emask, lane != i1))
    e2 = jnp.exp(v2 - v1)
    g1 = p_sel / (1.0 + e2)
    g2 = p_sel * e2 / (1.0 + e2)
    out = jnp.where(lane == 0, (i1 - n_groups).astype(F32), 0.0)
    out = jnp.where(lane == 1, (i2 - n_groups).astype(F32), out)
    out = jnp.where(lane == 2, g1, out)
    return jnp.where(lane == 3, g2, out)


def _merge_kernel(osb_ref, od_ref, h_ref, wo_ref, g_ref, b_ref, wr_ref, br_ref, *rest,
                  alpha, n_groups, per_group, precision, aliased, n_tiles):
    h1_ref, r_ref = rest[2:] if aliased else rest
    sbw = osb_ref.shape[1]

    @pl.when(pl.program_id(0) < n_tiles)
    def _():
        att = (jnp.dot(osb_ref[...], wo_ref[:sbw], precision=precision, preferred_element_type=F32)
               + jnp.dot(od_ref[...], wo_ref[sbw:], precision=precision, preferred_element_type=F32))
        h1 = _layer_norm(alpha * h_ref[...] + att, g_ref[...], b_ref[...])
        h1_ref[...] = h1
        logits = jnp.dot(h1, wr_ref[...], precision=HIGHEST, preferred_element_type=F32)
        r_ref[...] = _route(logits, br_ref[...], n_groups, per_group)

    @pl.when(pl.program_id(0) >= n_tiles)
    def _():
        h1_ref[...] = jnp.zeros_like(h1_ref)
        r_ref[...] = jnp.zeros_like(r_ref)


def _merge(osb, od, h, wo, g, b, wr, br, total_rows, tile_offset, precision, prev, alpha, n_groups,
           per_group):
    n, d = h.shape
    tm = ROW_TILE
    assert n % tm == 0 and total_rows % tm == 0
    n_tiles = n // tm
    grid = n_tiles if prev is not None else total_rows // tm
    row = lambda wd: pl.BlockSpec((tm, wd), lambda i: (jnp.minimum(i, n_tiles - 1), 0))
    const = lambda a: pl.BlockSpec(a.shape, lambda i: (0,) * a.ndim, pipeline_mode=pl.Buffered(1))
    out_row = lambda wd: pl.BlockSpec((tm, wd), lambda i: (i + tile_offset, 0))
    args = [osb, od, h, wo, g, b, wr, br]
    in_specs = [row(osb.shape[1]), row(od.shape[1]), row(d), const(wo), const(g), const(b), const(wr),
                const(br)]
    aliases = {}
    if prev is not None:
        args += list(prev)
        in_specs += [pl.BlockSpec(memory_space=pl.ANY)] * 2
        aliases = {8: 0, 9: 1}
    return pl.pallas_call(
        functools.partial(_merge_kernel, alpha=alpha, n_groups=n_groups, per_group=per_group,
                          precision=precision, aliased=prev is not None, n_tiles=n_tiles),
        grid=(grid,),
        in_specs=in_specs,
        out_specs=(out_row(d), out_row(LANES)),
        out_shape=(jax.ShapeDtypeStruct((total_rows, d), F32),
                   jax.ShapeDtypeStruct((total_rows, LANES), F32)),
        input_output_aliases=aliases,
        compiler_params=_params(("arbitrary",)),
    )(*args)


def _moe_kernel(blk_exp_ref, n_used_ref, n_valid_ref, slot_tok_ref, dst_row_ref, x_hbm, wgu_ref, wd_ref,
                y_hbm, xbuf, ybuf, gsem, ssem):
    i = pl.program_id(0)
    d_exp = wd_ref.shape[1]
    n_used = n_used_ref[0]
    slot = i % 2

    def gather_row(blk, s, r):
        tok = slot_tok_ref[blk * ROW_TILE + r]
        return pltpu.make_async_copy(x_hbm.at[pl.ds(tok, 1)], xbuf.at[s, pl.ds(r, 1)], gsem.at[s])

    def scatter_row(blk, s, r):
        dst = dst_row_ref[blk * ROW_TILE + r]
        return pltpu.make_async_copy(ybuf.at[s, pl.ds(r, 1)], y_hbm.at[pl.ds(dst, 1)], ssem.at[s])

    def start_rows(copy_row, blk, s):
        def issue(r, carry):
            copy_row(blk, s, r).start()
            return carry
        lax.fori_loop(0, n_valid_ref[blk], issue, 0)

    def wait_rows(copy_row, blk, s):
        def done(r, carry):
            copy_row(blk, s, r).wait()
            return carry
        lax.fori_loop(0, n_valid_ref[blk], done, 0)

    @pl.when(i == 0)
    def _():
        xbuf[...] = jnp.zeros_like(xbuf)
        start_rows(gather_row, 0, 0)

    @pl.when(i < n_used)
    def _():
        @pl.when(i + 1 < n_used)
        def _():
            start_rows(gather_row, i + 1, 1 - slot)

        wait_rows(gather_row, i, slot)

        @pl.when(i >= 2)
        def _():
            wait_rows(scatter_row, i - 2, slot)

        x = xbuf[slot].astype(BF16)
        gu = jnp.dot(x, wgu_ref[0], preferred_element_type=F32)
        gate, up = gu[:, :d_exp], gu[:, d_exp:]
        hmid = (gate * jax.nn.sigmoid(gate) * up).astype(BF16)
        ybuf[slot] = jnp.dot(hmid, wd_ref[0], preferred_element_type=F32)
        start_rows(scatter_row, i, slot)

    @pl.when(i == pl.num_programs(0) - 1)
    def _():
        wait_rows(scatter_row, n_used - 1, (n_used - 1) % 2)

        @pl.when(n_used >= 2)
        def _():
            wait_rows(scatter_row, n_used - 2, n_used % 2)


def _moe(blk_exp, n_used, n_valid, slot_tok, dst_row, x_all, wgu16, wd16, n_out_rows):
    n_blocks = blk_exp.shape[0]
    tm = ROW_TILE
    d = x_all.shape[1]
    grid_spec = pltpu.PrefetchScalarGridSpec(
        num_scalar_prefetch=5,
        grid=(n_blocks,),
        in_specs=[pl.BlockSpec(memory_space=pl.ANY),
                  pl.BlockSpec((1,) + wgu16.shape[1:], lambda i, be, *_: (be[i], 0, 0)),
                  pl.BlockSpec((1,) + wd16.shape[1:], lambda i, be, *_: (be[i], 0, 0))],
        out_specs=pl.BlockSpec(memory_space=pl.ANY),
        scratch_shapes=[pltpu.VMEM((2, tm, d), F32), pltpu.VMEM((2, tm, d), F32),
                        pltpu.SemaphoreType.DMA((2,)), pltpu.SemaphoreType.DMA((2,))],
    )
    return pl.pallas_call(
        _moe_kernel,
        grid_spec=grid_spec,
        out_shape=jax.ShapeDtypeStruct((n_out_rows, d), F32),
        compiler_params=_params(("arbitrary",)),
    )(blk_exp, n_used, n_valid, slot_tok, dst_row, x_all, wgu16, wd16)


def _dispatch_tables(expert_idx, n_experts, tm):
    n_tok = expert_idx.shape[0]
    n_assign = n_tok * TOP_K
    flat_e = expert_idx.reshape(-1)
    order = jnp.argsort(flat_e).astype(I32)
    sorted_e = flat_e[order]
    counts = jnp.bincount(flat_e, length=n_experts).astype(I32)
    padded = (counts + tm - 1) // tm * tm
    pad_end = jnp.cumsum(padded)
    pad_start = pad_end - padded
    start = jnp.cumsum(counts) - counts
    slot = pad_start[sorted_e] + jnp.arange(n_assign, dtype=I32) - start[sorted_e]
    n_blocks = -(-n_assign // tm) + n_experts
    n_slots = n_blocks * tm
    slot_tok = jnp.zeros((n_slots,), I32).at[slot].set(order // TOP_K)
    dst_row = jnp.zeros((n_slots,), I32).at[slot].set(order)
    n_valid = jnp.zeros((n_slots,), I32).at[slot].set(1).reshape(n_blocks, tm).sum(axis=1)
    n_used = (pad_end[-1] // tm).astype(I32)
    blk = jnp.minimum(jnp.arange(n_blocks, dtype=I32), n_used - 1)
    blk_exp = jnp.minimum(jnp.searchsorted(pad_end, blk * tm, side='right'), n_experts - 1).astype(I32)
    return blk_exp, n_used.reshape(1), n_valid, slot_tok, dst_row


def _combine_kernel(y_ref, r_ref, h_ref, g_ref, b_ref, o_ref, *, alpha):
    d = h_ref.shape[1]
    r = r_ref[...]
    ffn = y_ref[:, :d] * r[:, 2:3] + y_ref[:, d:] * r[:, 3:4]
    o_ref[...] = _layer_norm(alpha * h_ref[...] + ffn, g_ref[...], b_ref[...])


def _combine(y_pairs, r_all, h_all, g, b, alpha):
    n, d = h_all.shape
    tm = ROW_TILE
    row = lambda wd: pl.BlockSpec((tm, wd), lambda i: (i, 0))
    const = lambda a: pl.BlockSpec(a.shape, lambda i: (0,) * a.ndim)
    return pl.pallas_call(
        functools.partial(_combine_kernel, alpha=alpha),
        grid=(n // tm,),
        in_specs=[row(2 * d), row(LANES), row(d), const(g), const(b)],
        out_specs=row(d),
        out_shape=jax.ShapeDtypeStruct((n, d), F32),
        compiler_params=_params(("arbitrary",)),
    )(y_pairs, r_all, h_all, g, b)


def kernel(x_prompt, x_sample, cache_k_sb, cache_v_sb, cache_k_diff, cache_v_diff, page_table, meta_tokens, ln_in_g, ln_in_b, rel_bias, w_in, sb_norm_g, diff_lambda, diff_subln_g, w_out, ln1_g, ln1_b, w_group, b_group, w_expert_router, b_expert_router, w_gate_up, w_down, ln2_g, ln2_b):
    bsz, seq, d = x_prompt.shape
    db = x_sample.shape[0]
    assert x_sample.shape[1] == 1
    depth = w_in.shape[0]
    assert depth == 1, "the prompt/decode buffers below are laid out for a single layer"
    n_meta = meta_tokens.shape[0]
    hd = sb_norm_g.shape[-1]
    sb_kv, diff_kv = cache_k_sb.shape[3], cache_k_diff.shape[3]
    sb_w, diff_w = d // 2, d // 2
    sb_kv_w, diff_kv_w = sb_kv * hd, diff_kv * 2 * hd
    splits = [0, sb_w, sb_w + sb_kv_w, sb_w + 2 * sb_kv_w, sb_w + 2 * sb_kv_w + diff_w,
              sb_w + 2 * sb_kv_w + diff_w + diff_kv_w, sb_w + 2 * sb_kv_w + diff_w + 2 * diff_kv_w]
    assert splits[-1] == w_in.shape[-1]
    n_groups = w_group.shape[-1]
    n_experts = w_expert_router.shape[-1]
    per_group = n_experts // n_groups
    page = cache_k_sb.shape[2]
    past = page_table.shape[1] * page
    alpha = (2 * depth) ** 0.25
    li = 0
    lam_init = 0.8 - 0.6 * math.exp(-0.3 * li)

    seq_len = n_meta + seq
    lp = -(-seq_len // (LANES * PROJ_TILE // math.gcd(LANES, PROJ_TILE))) * (LANES * PROJ_TILE // math.gcd(LANES, PROJ_TILE))
    xpad = jnp.concatenate([jnp.broadcast_to(meta_tokens[None], (bsz, n_meta, d)), x_prompt,
                            jnp.zeros((bsz, lp - seq_len, d), F32)], axis=1)
    g_in, b_in = ln_in_g.reshape(1, d), ln_in_b.reshape(1, d)
    w_in16 = w_in[li].astype(BF16)
    w_out16 = w_out[li].astype(BF16)
    wgu16 = w_gate_up[li].astype(BF16)
    wd16 = w_down[li].astype(BF16)
    w_router = jnp.zeros((d, LANES), F32).at[:, :n_groups].set(w_group[li]) \
        .at[:, n_groups:n_groups + n_experts].set(w_expert_router[li])
    b_router = jnp.zeros((1, LANES), F32).at[0, :n_groups].set(b_group[li]) \
        .at[0, n_groups:n_groups + n_experts].set(b_expert_router[li])
    g1, b1 = ln1_g[li].reshape(1, d), ln1_b[li].reshape(1, d)
    g2, b2 = ln2_g[li].reshape(1, d), ln2_b[li].reshape(1, d)
    g_sb = sb_norm_g[li].reshape(1, hd)
    g_diff = diff_subln_g[li].reshape(1, 2 * hd)
    lam_params = diff_lambda[li]
    idx = jnp.arange(MXU_DIM)
    tri_half = (idx[:LANES, None] >= idx[None, :LANES])
    tri = jnp.concatenate([tri_half, jnp.ones((LANES, LANES), bool)], axis=1).astype(BF16)
    tri = jnp.concatenate([tri, tri], axis=0)

    near, far, dec = _bias_tables(rel_bias, past)

    (hp, qsb16, qd16, ksb16, vsb16, kd16, vd16, k_sb, v_sb, k_d, v_d) = _ln_proj_prompt(
        xpad, g_in, b_in, w_in16, splits, seq_len)
    o_sb = _sb_prompt(qsb16, ksb16, vsb16, tri, g_sb, sb_kv)
    o_d = _diff_prompt(qd16, kd16, vd16, near, far, lam_params, g_diff, diff_kv, lam_init)
    n_prompt = bsz * lp
    total_rows = n_prompt + ROW_TILE
    assert n_prompt % ROW_TILE == 0 and db <= ROW_TILE
    h1_all, r_all = _merge(o_sb.reshape(n_prompt, sb_w), o_d.reshape(n_prompt, diff_w),
                           hp.reshape(n_prompt, d), w_out16, g1, b1, w_router, b_router, total_rows, 0,
                           None, None, alpha, n_groups, per_group)

    hs, proj_s = _ln_proj_rows(x_sample.reshape(db, d), g_in, b_in, w_in[li])
    qn_sb, kn_sb, vn_sb, qn_d, kn_d, vn_d = [proj_s[:, splits[t]:splits[t + 1]] for t in range(6)]
    pt_flat = page_table.reshape(-1).astype(I32)
    os_sb = _sb_decode(pt_flat, qn_sb, tri, g_sb, cache_k_sb, cache_v_sb, li)
    n_dh = diff_w // (2 * hd)
    dec_rows = jnp.concatenate([dec[:, 0, :]] * 2, axis=0)
    bias0 = jnp.concatenate([rel_bias[0] * LOG2E] * 2).reshape(2 * n_dh, 1)
    os_d = _diff_decode(pt_flat, qn_d, kn_d, vn_d, dec_rows, bias0, lam_params, g_diff, cache_k_diff,
                        cache_v_diff, li, lam_init)
    pad_rows = lambda a: jnp.concatenate([a, jnp.zeros((ROW_TILE - db, a.shape[1]), a.dtype)], axis=0)
    h1_all, r_all = _merge(pad_rows(os_sb), pad_rows(os_d), pad_rows(hs), w_out[li], g1, b1, w_router,
                           b_router, total_rows, n_prompt // ROW_TILE, HIGHEST, (h1_all, r_all), alpha,
                           n_groups, per_group)

    expert_idx = r_all[:, :TOP_K].astype(I32)
    tables = _dispatch_tables(expert_idx, n_experts, ROW_TILE)
    y_rows = _moe(*tables, h1_all, wgu16, wd16, total_rows * TOP_K)
    y_all = _combine(y_rows.reshape(total_rows, TOP_K * d), r_all, h1_all, g2, b2, alpha)

    y_prompt = y_all[:n_prompt].reshape(bsz, lp, d)[:, n_meta:seq_len]
    y_sample = y_all[n_prompt:n_prompt + db].reshape(db, 1, d)
    return (y_prompt, y_sample,
            k_sb.reshape(1, bsz, seq_len, sb_kv, hd), v_sb.reshape(1, bsz, seq_len, sb_kv, hd),
            k_d.reshape(1, bsz, seq_len, diff_kv, 2 * hd), v_d.reshape(1, bsz, seq_len, diff_kv, 2 * hd),
            kn_sb.reshape(1, db, 1, sb_kv, hd), vn_sb.reshape(1, db, 1, sb_kv, hd),
            kn_d.reshape(1, db, 1, diff_kv, 2 * hd), vn_d.reshape(1, db, 1, diff_kv, 2 * hd))
```

```python
import functools
import math

import jax
import jax.numpy as jnp
from jax import lax
from jax.experimental import pallas as pl
from jax.experimental.pallas import tpu as pltpu

F32 = jnp.float32
BF16 = jnp.bfloat16
I32 = jnp.int32

LANES = 128
MXU_DIM = 256
LN_EPS = 1e-5
REL_BUCKETS, REL_MAX_EXACT, REL_MAX_DIST = 32, 16, 128
TOP_K = 2
TOP_K_SHIFT = TOP_K.bit_length() - 1
LOG2E = math.log2(math.e)
MASKED = -1e30
SB_UNDERFLOW = -104.0
ROW_TILE = 256
PROJ_TILE = 320
DEC_PAGES = 16
FAR_TILES = 4
VMEM_LIMIT = 56 * 1024 * 1024

NT_DIMS = (((1,), (1,)), ((), ()))


def _params(sem, vmem=VMEM_LIMIT):
    return pltpu.CompilerParams(dimension_semantics=sem, vmem_limit_bytes=vmem)


def _layer_norm(x, g, b):
    mu = jnp.mean(x, axis=-1, keepdims=True)
    xc = x - mu
    var = jnp.mean(xc * xc, axis=-1, keepdims=True)
    return xc * lax.rsqrt(var + LN_EPS) * g + b


def _rms_norm(x, g):
    return x * lax.rsqrt(jnp.mean(x * x, axis=-1, keepdims=True) + LN_EPS) * g


def _split_bf16(x):
    hi = x.astype(BF16)
    lo = (x - hi.astype(F32)).astype(BF16)
    return hi, lo


def _lambda_value(lp, lam_init):
    a = jnp.sum(lp[0:1] * lp[1:2], axis=-1, keepdims=True)
    b = jnp.sum(lp[2:3] * lp[3:4], axis=-1, keepdims=True)
    return jnp.exp(a) - jnp.exp(b) + lam_init


def _ln_proj_prompt_kernel(x_ref, g_ref, b_ref, w_ref, h_ref, qsb_ref, qd_ref,
                           ksb16_ref, vsb16_ref, kd16_ref, vd16_ref,
                           ksb_ref, vsb_ref, kd_ref, vd_ref, *, splits):
    h = _layer_norm(x_ref[0], g_ref[...], b_ref[...])
    h_ref[0] = h
    hb = h.astype(BF16)

    def proj(seg):
        lo, hi = splits[seg], splits[seg + 1]
        return jnp.dot(hb, w_ref[:, lo:hi], preferred_element_type=F32)

    qsb_ref[0] = proj(0).astype(BF16)
    for seg, full_ref, half_ref in ((1, ksb_ref, ksb16_ref), (2, vsb_ref, vsb16_ref)):
        y = proj(seg)
        full_ref[0] = y
        half_ref[0] = y.astype(BF16)
    qd_ref[0] = proj(3).astype(BF16)
    for seg, full_ref, half_ref in ((4, kd_ref, kd16_ref), (5, vd_ref, vd16_ref)):
        y = proj(seg)
        full_ref[0] = y
        half_ref[0] = y.astype(BF16)


def _ln_proj_prompt(xpad, g, b, w16, splits, seq_len):
    bsz, lp, d = xpad.shape
    widths = [splits[i + 1] - splits[i] for i in range(6)]
    tm = PROJ_TILE
    assert lp % tm == 0 and lp - seq_len < tm
    row = lambda wd: pl.BlockSpec((1, tm, wd), lambda bi, i: (bi, i, 0))
    const = lambda shp: pl.BlockSpec(shp, lambda bi, i: (0,) * len(shp))
    out_shape = (
        jax.ShapeDtypeStruct((bsz, lp, d), F32),
        jax.ShapeDtypeStruct((bsz, lp, widths[0]), BF16),
        jax.ShapeDtypeStruct((bsz, lp, widths[3]), BF16),
        jax.ShapeDtypeStruct((bsz, lp, widths[1]), BF16),
        jax.ShapeDtypeStruct((bsz, lp, widths[2]), BF16),
        jax.ShapeDtypeStruct((bsz, lp, widths[4]), BF16),
        jax.ShapeDtypeStruct((bsz, lp, widths[5]), BF16),
        jax.ShapeDtypeStruct((bsz, seq_len, widths[1]), F32),
        jax.ShapeDtypeStruct((bsz, seq_len, widths[2]), F32),
        jax.ShapeDtypeStruct((bsz, seq_len, widths[4]), F32),
        jax.ShapeDtypeStruct((bsz, seq_len, widths[5]), F32),
    )
    out_specs = (row(d), row(widths[0]), row(widths[3]), row(widths[1]), row(widths[2]),
                 row(widths[4]), row(widths[5]), row(widths[1]), row(widths[2]), row(widths[4]),
                 row(widths[5]))
    return pl.pallas_call(
        functools.partial(_ln_proj_prompt_kernel, splits=tuple(splits)),
        grid=(bsz, lp // tm),
        in_specs=[row(d), const((1, d)), const((1, d)),
                  pl.BlockSpec(w16.shape, lambda bi, i: (0, 0), pipeline_mode=pl.Buffered(1))],
        out_specs=out_specs,
        out_shape=out_shape,
        compiler_params=_params(("arbitrary", "arbitrary")),
    )(xpad, g, b, w16)


def _ln_proj_rows_kernel(x_ref, g_ref, b_ref, w_ref, h_ref, p_ref):
    h = _layer_norm(x_ref[...], g_ref[...], b_ref[...])
    h_ref[...] = h
    p_ref[...] = jnp.dot(h.astype(BF16), w_ref[...], preferred_element_type=F32)


def _ln_proj_rows(x, g, b, w):
    n, d = x.shape
    pw = w.shape[1]
    tn = 512
    assert pw % tn == 0
    return pl.pallas_call(
        _ln_proj_rows_kernel,
        grid=(pw // tn,),
        in_specs=[pl.BlockSpec((n, d), lambda j: (0, 0)), pl.BlockSpec((1, d), lambda j: (0, 0)),
                  pl.BlockSpec((1, d), lambda j: (0, 0)), pl.BlockSpec((d, tn), lambda j: (0, j))],
        out_specs=(pl.BlockSpec((n, d), lambda j: (0, 0)), pl.BlockSpec((n, tn), lambda j: (0, j))),
        out_shape=(jax.ShapeDtypeStruct((n, d), F32), jax.ShapeDtypeStruct((n, pw), F32)),
        compiler_params=_params(("arbitrary",)),
    )(x, g, b, w)


def _bias_from_dist(dist, rel_ref, head):
    n = jnp.maximum(dist, 0)
    nf = jnp.maximum(n, 1).astype(F32)
    large = REL_MAX_EXACT + (jnp.log(nf / REL_MAX_EXACT) / math.log(REL_MAX_DIST / REL_MAX_EXACT)
                             * (REL_BUCKETS - REL_MAX_EXACT)).astype(I32)
    large = jnp.minimum(large, REL_BUCKETS - 1)
    bucket = jnp.where(n < REL_MAX_EXACT, n, large)
    out = jnp.zeros(dist.shape, F32)
    for bkt in range(REL_BUCKETS):
        out = jnp.where(bucket == bkt, rel_ref[bkt, head], out)
    return out * LOG2E


def _bias_tables_kernel(rel_ref, near_ref, far_ref, dec_ref, *, n_heads, past):
    i = lax.broadcasted_iota(I32, (LANES, 2 * LANES), 0)
    j = lax.broadcasted_iota(I32, (LANES, 2 * LANES), 1)
    dist = i + LANES - j
    kpos = lax.broadcasted_iota(I32, dec_ref.shape[1:], 1)
    for h in range(n_heads):
        near_ref[h] = jnp.where(dist >= 0, _bias_from_dist(dist, rel_ref, h), MASKED)
        far_ref[h] = jnp.full(far_ref.shape[1:], rel_ref[REL_BUCKETS - 1, h] * LOG2E, F32)
        dec_ref[h] = _bias_from_dist(past - kpos, rel_ref, h)


def _bias_tables(rel_bias, past):
    n_heads = rel_bias.shape[1]
    return pl.pallas_call(
        functools.partial(_bias_tables_kernel, n_heads=n_heads, past=past),
        in_specs=[pl.BlockSpec(memory_space=pltpu.SMEM)],
        out_shape=(jax.ShapeDtypeStruct((n_heads, LANES, 2 * LANES), F32),
                   jax.ShapeDtypeStruct((n_heads, LANES, LANES), F32),
                   jax.ShapeDtypeStruct((n_heads, 8, past), F32)),
    )(rel_bias)


def _sb_tile(z, vis, tri, carry):
    l = -(jnp.maximum(z, 0.0) + jnp.log(1.0 + jnp.exp(-jnp.abs(z))))
    if vis is not None:
        l = jnp.where(vis, l, 0.0)
    hi, lo = _split_bf16(l)
    t2 = jnp.dot(jnp.concatenate([hi, lo], axis=1), tri, preferred_element_type=F32)
    nt = z.shape[1]
    incl, total = t2[:, :nt], t2[:, nt:]
    w = jnp.exp(z + incl + carry)
    if vis is not None:
        w = jnp.where(vis, w, 0.0)
    return w, carry + total


def _sb_prompt_kernel(q_ref, k_ref, v_ref, tri_ref, g_ref, o_ref, acc_ref, carry_ref, *, scale):
    qi = pl.program_id(1)
    n_kv, m, hd = acc_ref.shape
    group = m // LANES
    q = q_ref[0]
    qh = [jnp.concatenate([q[:, (h * group + g) * hd:(h * group + g + 1) * hd] for g in range(group)], axis=0)
          for h in range(n_kv)]
    acc_ref[...] = jnp.zeros_like(acc_ref)
    carry_ref[...] = jnp.zeros_like(carry_ref)
    rowpos = lax.broadcasted_iota(I32, (m, LANES), 0) & (LANES - 1)
    col = lax.broadcasted_iota(I32, (m, LANES), 1)
    tri = tri_ref[...]

    def cond(state):
        j, live = state
        return jnp.logical_and(j >= 0, live > SB_UNDERFLOW)

    def body(state):
        j, _ = state
        ks = pl.multiple_of(j * LANES, LANES)
        vis = col < rowpos + (qi - j) * LANES
        live = None
        for h in range(n_kv):
            k = k_ref[0, pl.ds(ks, LANES), h * hd:(h + 1) * hd]
            v = v_ref[0, pl.ds(ks, LANES), h * hd:(h + 1) * hd]
            z = lax.dot_general(qh[h], k, NT_DIMS, preferred_element_type=F32) * scale
            w, carry = _sb_tile(z, vis, tri, carry_ref[h])
            acc_ref[h] += jnp.dot(w.astype(BF16), v, preferred_element_type=F32)
            carry_ref[h] = carry
            live = carry if live is None else jnp.maximum(live, carry)
        return j - 1, jnp.max(live)

    lax.while_loop(cond, body, (qi, jnp.float32(0.0)))
    outs = []
    for h in range(n_kv):
        o = _rms_norm(acc_ref[h], g_ref[...]).astype(o_ref.dtype)
        outs += [o[g * LANES:(g + 1) * LANES] for g in range(group)]
    o_ref[0] = jnp.concatenate(outs, axis=1)


def _sb_prompt(q16, k16, v16, tri, g, n_kv):
    bsz, lp, width = q16.shape
    kvw = k16.shape[2]
    hd = kvw // n_kv
    m = width // n_kv // hd * LANES
    resident = lambda: pl.BlockSpec((1, lp, kvw), lambda b, i: (b, 0, 0), pipeline_mode=pl.Buffered(1))
    return pl.pallas_call(
        functools.partial(_sb_prompt_kernel, scale=hd ** -0.5),
        grid=(bsz, lp // LANES),
        in_specs=[pl.BlockSpec((1, LANES, width), lambda b, i: (b, i, 0)),
                  resident(), resident(),
                  pl.BlockSpec(tri.shape, lambda b, i: (0, 0)),
                  pl.BlockSpec((1, hd), lambda b, i: (0, 0))],
        out_specs=pl.BlockSpec((1, LANES, width), lambda b, i: (b, i, 0)),
        out_shape=jax.ShapeDtypeStruct((bsz, lp, width), BF16),
        scratch_shapes=[pltpu.VMEM((n_kv, m, hd), F32), pltpu.VMEM((n_kv, m, LANES), F32)],
        compiler_params=_params(("arbitrary",) * 2),
    )(q16, k16, v16, tri, g)


def _sb_decode_kernel(pt_ref, q_ref, tri_ref, g_ref, k_hbm, v_hbm, o_ref, kbuf, vbuf, sem, *,
                      scale, n_kv, n_pages, layer):
    s = pl.program_id(0)
    q = q_ref[0]
    hd = g_ref.shape[1]
    n_heads = q.shape[1] // hd
    group = n_heads // n_kv
    zeros = jnp.zeros((1, hd), F32)
    rows = []
    for n in range(n_heads):
        seg = q[:, n * hd:(n + 1) * hd]
        rows.append(jnp.concatenate([seg if h == n // group else zeros for h in range(n_kv)], axis=1))
    qb = jnp.concatenate(rows, axis=0)
    q16 = qb.astype(BF16)
    tri = tri_ref[...]

    def page_copies(j, slot):
        page = pt_ref[s * n_pages + j]
        return [pltpu.make_async_copy(hbm.at[layer, page, :, h, :], buf.at[slot, h], sem.at[which, slot])
                for which, (hbm, buf) in enumerate(((k_hbm, kbuf), (v_hbm, vbuf))) for h in range(n_kv)]

    def fetch(j, slot):
        for cp in page_copies(j, slot):
            cp.start()

    def wait_page(j, slot):
        for cp in page_copies(j, slot):
            cp.wait()

    def heads_on_lanes(buf, slot):
        return jnp.concatenate([buf[slot, h] for h in range(n_kv)], axis=1)

    fetch(n_pages - 1, 0)

    def cond(state):
        j, live, _, _ = state
        return jnp.logical_and(j >= 0, live > SB_UNDERFLOW)

    def body(state):
        j, _, carry, acc = state
        slot = (n_pages - 1 - j) % 2
        wait_page(j, slot)

        @pl.when(j > 0)
        def _():
            fetch(j - 1, 1 - slot)

        k = heads_on_lanes(kbuf, slot).astype(BF16)
        v = heads_on_lanes(vbuf, slot).astype(BF16)
        z = lax.dot_general(q16, k, NT_DIMS, preferred_element_type=F32) * scale
        w, carry = _sb_tile(z, None, tri, carry)
        acc = acc + jnp.dot(w.astype(BF16), v, preferred_element_type=F32)
        return j - 1, jnp.max(carry), carry, acc

    init = (jnp.int32(n_pages - 1), jnp.float32(0.0), jnp.zeros((n_heads, LANES), F32),
            jnp.zeros((n_heads, n_kv * hd), F32))
    j_end, _, _, acc = lax.while_loop(cond, body, init)

    @pl.when(j_end >= 0)
    def _():
        wait_page(j_end, (n_pages - 1 - j_end) % 2)

    outs = []
    for n in range(n_heads):
        h = n // group
        outs.append(_rms_norm(acc[n:n + 1, h * hd:(h + 1) * hd], g_ref[...]))
    o_ref[0] = jnp.concatenate(outs, axis=1)


def _sb_decode(pt_flat, q, tri, g, cache_k, cache_v, layer):
    db, width = q.shape
    depth, n_pool, ps, n_kv, hd = cache_k.shape
    assert ps == LANES
    n_pages = pt_flat.shape[0] // db
    grid_spec = pltpu.PrefetchScalarGridSpec(
        num_scalar_prefetch=1,
        grid=(db,),
        in_specs=[pl.BlockSpec((1, 1, width), lambda s, pt: (s, 0, 0)),
                  pl.BlockSpec(tri.shape, lambda s, pt: (0, 0)),
                  pl.BlockSpec((1, hd), lambda s, pt: (0, 0)),
                  pl.BlockSpec(memory_space=pl.ANY), pl.BlockSpec(memory_space=pl.ANY)],
        out_specs=pl.BlockSpec((1, 1, width), lambda s, pt: (s, 0, 0)),
        scratch_shapes=[pltpu.VMEM((2, n_kv, ps, hd), F32), pltpu.VMEM((2, n_kv, ps, hd), F32),
                        pltpu.SemaphoreType.DMA((2, 2))],
    )
    out = pl.pallas_call(
        functools.partial(_sb_decode_kernel, scale=hd ** -0.5, n_kv=n_kv, n_pages=n_pages, layer=layer),
        grid_spec=grid_spec,
        out_shape=jax.ShapeDtypeStruct((db, 1, width), F32),
        compiler_params=_params(("arbitrary",)),
    )(pt_flat, q.reshape(db, 1, width), tri, g, cache_k, cache_v)
    return out.reshape(db, width)


def _diff_prompt_kernel(q_ref, k_ref, v_ref, near_ref, far_ref, lam_ref, g_ref, o_ref,
                        m_ref, l_ref, acc_ref, *, scale, lam_init):
    qi = pl.program_id(2)
    q = q_ref[0]
    hd = k_ref.shape[2] // 2
    group = q.shape[1] // (2 * hd)
    rows = group * LANES
    qm = [jnp.concatenate([q[:, (2 * g + c) * hd:(2 * g + c + 1) * hd] for g in range(group)], axis=0)
          for c in range(2)]
    c1 = scale * LOG2E

    def scores(c, ks, width):
        k = k_ref[0, pl.ds(ks, width), c * hd:(c + 1) * hd]
        return lax.dot_general(qm[c], k, NT_DIMS, preferred_element_type=F32) * c1

    def near_chunk(ks, bias, width):
        v = v_ref[0, pl.ds(ks, width), :]
        for c in range(2):
            s = scores(c, ks, width) + bias
            mx = jnp.max(s, axis=-1, keepdims=True)
            p = jnp.exp2(s - mx)
            m_ref[0, c] = jnp.broadcast_to(mx, (rows, LANES))
            l_ref[0, c] = jnp.broadcast_to(jnp.sum(p, axis=-1, keepdims=True), (rows, LANES))
            acc_ref[0, c] = jnp.dot(p.astype(BF16), v, preferred_element_type=F32)

    @pl.when(qi == 0)
    def _():
        near_chunk(0, near_ref[0, :, LANES:], LANES)

    @pl.when(qi > 0)
    def _():
        near_chunk(pl.multiple_of((qi - 1) * LANES, LANES), near_ref[0], 2 * LANES)

    m_ref[1] = jnp.full(m_ref.shape[1:], MASKED, F32)
    l_ref[1] = jnp.zeros(l_ref.shape[1:], F32)
    acc_ref[1] = jnp.zeros(acc_ref.shape[1:], F32)

    def far_chunk(ks, stream, key_range=None):
        v = v_ref[0, pl.ds(ks, wide), :]
        fb = far_ref[0]
        for c in range(2):
            s = scores(c, ks, wide)
            if key_range is not None:
                kpos = ks + lax.broadcasted_iota(I32, s.shape, 1)
                s = jnp.where(jnp.logical_and(kpos >= key_range[0], kpos < key_range[1]), s, MASKED)
            m_old = m_ref[stream, c]
            m_new = jnp.maximum(m_old, jnp.max(s, axis=-1, keepdims=True) + fb)
            p = jnp.exp2(s - (m_new - fb)[:, :1])
            pv = jnp.dot(p.astype(BF16), v, preferred_element_type=F32)
            alpha = jnp.exp2(m_old - m_new)
            l_ref[stream, c] = alpha * l_ref[stream, c] + jnp.sum(p, axis=-1, keepdims=True)
            acc_ref[stream, c] = alpha[:, :1] * acc_ref[stream, c] + pv
            m_ref[stream, c] = m_new

    n_far = jnp.maximum(qi - 1, 0)
    wide = FAR_TILES * LANES
    n_full = n_far // FAR_TILES

    def pair_body(t, carry):
        far_chunk(pl.multiple_of(2 * t * wide, wide), 0)
        far_chunk(pl.multiple_of((2 * t + 1) * wide, wide), 1)
        return carry

    lax.fori_loop(0, n_full // 2, pair_body, 0)

    @pl.when(n_full % 2 == 1)
    def _():
        far_chunk(pl.multiple_of((n_full - 1) * wide, wide), 0)

    @pl.when(n_far % FAR_TILES != 0)
    def _():
        ks = pl.multiple_of(jnp.minimum(n_full * wide, k_ref.shape[1] - wide), LANES)
        far_chunk(ks, 1, (n_full * wide, n_far * LANES))

    lam = _lambda_value(lam_ref[...], lam_init)
    merged = []
    for c in range(2):
        m = jnp.maximum(m_ref[0, c], m_ref[1, c])
        a0, a1 = jnp.exp2(m_ref[0, c] - m), jnp.exp2(m_ref[1, c] - m)
        l = a0 * l_ref[0, c] + a1 * l_ref[1, c]
        merged.append((a0[:, :1] * acc_ref[0, c] + a1[:, :1] * acc_ref[1, c]) / l[:, :1])
    a = merged[0] - lam * merged[1]
    o = _rms_norm(a, g_ref[...] * (1.0 - lam_init)).astype(o_ref.dtype)
    o_ref[0] = jnp.concatenate([o[g * LANES:(g + 1) * LANES] for g in range(group)], axis=1)


def _diff_prompt(q16, k16, v16, near, far, lam_params, g, n_kv, lam_init):
    bsz, lp, width = q16.shape
    gw = width // n_kv
    kw = k16.shape[2] // n_kv
    hd = kw // 2
    group = gw // kw
    rows = group * LANES
    nq = lp // LANES
    assert lp >= FAR_TILES * LANES
    near_g = near.reshape(n_kv, rows, 2 * LANES)
    far_g = far.reshape(n_kv, rows, LANES)
    return pl.pallas_call(
        functools.partial(_diff_prompt_kernel, scale=hd ** -0.5, lam_init=lam_init),
        grid=(bsz, n_kv, nq),
        in_specs=[pl.BlockSpec((1, LANES, gw), lambda b, h, i: (b, i, h)),
                  pl.BlockSpec((1, lp, kw), lambda b, h, i: (b, 0, h)),
                  pl.BlockSpec((1, lp, kw), lambda b, h, i: (b, 0, h)),
                  pl.BlockSpec((1, rows, 2 * LANES), lambda b, h, i: (h, 0, 0)),
                  pl.BlockSpec((1, rows, LANES), lambda b, h, i: (h, 0, 0)),
                  pl.BlockSpec(lam_params.shape, lambda b, h, i: (0, 0)),
                  pl.BlockSpec((1, kw), lambda b, h, i: (0, 0))],
        out_specs=pl.BlockSpec((1, LANES, gw), lambda b, h, i: (b, i, h)),
        out_shape=jax.ShapeDtypeStruct((bsz, lp, width), BF16),
        scratch_shapes=[pltpu.VMEM((2, 2, rows, LANES), F32), pltpu.VMEM((2, 2, rows, LANES), F32),
                        pltpu.VMEM((2, 2, rows, kw), F32)],
        compiler_params=_params(("arbitrary",) * 3),
    )(q16, k16, v16, near_g, far_g, lam_params, g)


def _diff_decode_kernel(pt_ref, q_ref, kn_ref, vn_ref, bias_ref, bias0_ref, lam_ref, g_ref, k_hbm, v_hbm,
                        o_ref, kbuf, vbuf, sem, s_ref, snew_ref, m_ref, a_ref, anew_ref, acc_ref, *,
                        scale, lam_init, n_pages, pages, layer, n_seq):
    s = pl.program_id(0)
    c = pl.program_id(1)
    nc = n_pages // pages
    n_kv, span, kw = kbuf.shape[1:]
    ps = span // pages
    hd = kw // 2
    group = q_ref.shape[2] // (n_kv * kw)
    step = s * nc + c
    slot = step % 2
    c1 = scale * LOG2E

    def chunk_copies(step_idx, dst_slot):
        seq, ch = step_idx // nc, step_idx % nc
        base_k = jnp.minimum(seq, n_seq - 1) * n_pages + ch * pages
        base_v = jnp.maximum(seq - 1, 0) * n_pages + ch * pages
        out = []
        for t in range(pages):
            for which, (hbm, buf, base) in enumerate(((k_hbm, kbuf, base_k), (v_hbm, vbuf, base_v))):
                page = pt_ref[base + t]
                out += [pltpu.make_async_copy(hbm.at[layer, page, :, h, :],
                                              buf.at[dst_slot, h, pl.ds(t * ps, ps)], sem.at[which, dst_slot])
                        for h in range(n_kv)]
        return out

    @pl.when(step == 0)
    def _():
        for cp in chunk_copies(step, slot):
            cp.start()

    @pl.when(step + 1 < (n_seq + 1) * nc)
    def _():
        for cp in chunk_copies(step + 1, 1 - slot):
            cp.start()

    for cp in chunk_copies(step, slot):
        cp.wait()

    lam = _lambda_value(lam_ref[...], lam_init)
    cols = pl.ds(pl.multiple_of(c * span, span), span)

    @pl.when(s < n_seq)
    def _():
        q = q_ref[0]
        zeros = jnp.zeros((1, hd), F32)
        for h in range(n_kv):
            rows = []
            for mp in range(2):
                for g in range(group):
                    lo = ((h * group + g) * 2 + mp) * hd
                    rows.append(jnp.concatenate([q[:, lo:lo + hd], zeros] if mp == 0 else
                                                [zeros, q[:, lo:lo + hd]], axis=1))
            qb = jnp.concatenate(rows, axis=0).astype(BF16)

            @pl.when(c == 0)
            def _():
                kn = kn_ref[0][:, h * kw:(h + 1) * kw].astype(BF16)
                s0 = jnp.sum(qb.astype(F32) * kn.astype(F32), axis=-1, keepdims=True) * c1 + bias0_ref[h]
                snew_ref[h] = jnp.broadcast_to(s0, snew_ref.shape[1:])
                m_ref[h] = jnp.broadcast_to(s0, m_ref.shape[1:])

            sc = lax.dot_general(qb, kbuf[slot, h].astype(BF16), NT_DIMS,
                                 preferred_element_type=F32) * c1 + bias_ref[h]
            s_ref[h, :, cols] = sc
            m_ref[h] = jnp.maximum(m_ref[h], jnp.max(sc, axis=-1, keepdims=True))

            @pl.when(c == nc - 1)
            def _():
                m = m_ref[h][:, :1]
                p = jnp.exp2(s_ref[h] - m)
                p0 = jnp.exp2(snew_ref[h][:, :1] - m)
                l = jnp.sum(p, axis=-1, keepdims=True) + p0
                pn, pn0 = p / l, p0 / l
                a_ref[s % 2, h] = (pn[:group] - lam * pn[group:]).astype(a_ref.dtype)
                anew_ref[s % 2, h] = jnp.broadcast_to(pn0[:group] - lam * pn0[group:], anew_ref.shape[2:])

    @pl.when(s >= 1)
    def _():
        par = (s - 1) % 2
        for h in range(n_kv):
            @pl.when(c == 0)
            def _():
                a0 = anew_ref[par, h][:, :1].astype(BF16).astype(F32)
                vn = vn_ref[0][:, h * kw:(h + 1) * kw].astype(BF16).astype(F32)
                acc_ref[h] = a0 * vn

            acc_ref[h] += jnp.dot(a_ref[par, h, :, cols], vbuf[slot, h].astype(BF16),
                                  preferred_element_type=F32)

            @pl.when(c == nc - 1)
            def _():
                gain = g_ref[...] * (1.0 - lam_init)
                for g in range(group):
                    lo = (h * group + g) * kw
                    o_ref[0, :, lo:lo + kw] = _rms_norm(acc_ref[h][g:g + 1], gain)


def _diff_decode(pt_flat, q, k_new, v_new, dec_bias, bias0, lam_params, g, cache_k, cache_v, layer,
                 lam_init):
    db, width = q.shape
    depth, n_pool, ps, n_kv, kw = cache_k.shape
    n_pages = pt_flat.shape[0] // db
    pages = min(DEC_PAGES, n_pages)
    assert n_pages % pages == 0
    group = width // (n_kv * kw)
    n_rows = 2 * group
    span = pages * ps
    past = n_pages * ps

    cur = lambda wd: pl.BlockSpec((1, 1, wd), lambda s, c, pt: (jnp.minimum(s, db - 1), 0, 0))
    prev = lambda wd: pl.BlockSpec((1, 1, wd), lambda s, c, pt: (jnp.maximum(s - 1, 0), 0, 0))
    const = lambda shp: pl.BlockSpec(shp, lambda s, c, pt: (0,) * len(shp))
    grid_spec = pltpu.PrefetchScalarGridSpec(
        num_scalar_prefetch=1,
        grid=(db + 1, n_pages // pages),
        in_specs=[cur(width), cur(n_kv * kw), prev(n_kv * kw),
                  pl.BlockSpec((n_kv, n_rows, span), lambda s, c, pt: (0, 0, c)),
                  const((n_kv, n_rows, 1)), const(lam_params.shape), const((1, kw)),
                  pl.BlockSpec(memory_space=pl.ANY), pl.BlockSpec(memory_space=pl.ANY)],
        out_specs=prev(width),
        scratch_shapes=[pltpu.VMEM((2, n_kv, span, kw), F32),
                        pltpu.VMEM((2, n_kv, span, kw), F32),
                        pltpu.SemaphoreType.DMA((2, 2)),
                        pltpu.VMEM((n_kv, n_rows, past), F32),
                        pltpu.VMEM((n_kv, n_rows, LANES), F32),
                        pltpu.VMEM((n_kv, n_rows, LANES), F32),
                        pltpu.VMEM((2, n_kv, group, past), BF16),
                        pltpu.VMEM((2, n_kv, group, LANES), F32),
                        pltpu.VMEM((n_kv, group, kw), F32)],
    )
    out = pl.pallas_call(
        functools.partial(_diff_decode_kernel, scale=(kw // 2) ** -0.5, lam_init=lam_init,
                          n_pages=n_pages, pages=pages, layer=layer, n_seq=db),
        grid_spec=grid_spec,
        out_shape=jax.ShapeDtypeStruct((db, 1, width), F32),
        compiler_params=_params(("arbitrary",) * 2),
    )(pt_flat, q.reshape(db, 1, width), k_new.reshape(db, 1, n_kv * kw), v_new.reshape(db, 1, n_kv * kw),
      dec_bias, bias0, lam_params, g, cache_k, cache_v)
    return out.reshape(db, width)


def _route(logits, bias, n_groups, per_group):
    x = logits + bias
    lane = lax.broadcasted_iota(I32, x.shape, 1).astype(F32)
    big = jnp.float32(4 * LANES)
    neg = jnp.float32(-jnp.inf)

    def first_argmax(vals, mask):
        mx = jnp.max(jnp.where(mask, vals, neg), axis=-1, keepdims=True)
        idx = jnp.min(jnp.where(jnp.logical_and(mask, vals == mx), lane, big), axis=-1, keepdims=True)
        return mx, idx

    gmask = lane < n_groups
    gmax, grp = first_argmax(x, gmask)
    p_sel = 1.0 / jnp.sum(jnp.where(gmask, jnp.exp(x - gmax), 0.0), axis=-1, keepdims=True)
    emask = jnp.logical_and(lane >= n_groups + grp * per_group, lane < n_groups + (grp + 1) * per_group)
    v1, i1 = first_argmax(x, emask)
    v2, i2 = first_argmax(x, jnp.logical_and(emask, lane != i1))
    e2 = jnp.exp(v2 - v1)
    g1 = p_sel / (1.0 + e2)
    g2 = p_sel * e2 / (1.0 + e2)
    out = jnp.where(lane == 0, (i1 - n_groups).astype(F32), 0.0)
    out = jnp.where(lane == 1, (i2 - n_groups).astype(F32), out)
    out = jnp.where(lane == 2, g1, out)
    return jnp.where(lane == 3, g2, out)


def _merge_kernel(osb_ref, od_ref, h_ref, wo_ref, g_ref, b_ref, wr_ref, br_ref, *rest,
                  alpha, n_groups, per_group, aliased, n_tiles):
    h1_ref, r_ref = rest[2:] if aliased else rest
    sbw = osb_ref.shape[1]

    @pl.when(pl.program_id(0) < n_tiles)
    def _():
        att = (jnp.dot(osb_ref[...].astype(BF16), wo_ref[:sbw], preferred_element_type=F32)
               + jnp.dot(od_ref[...].astype(BF16), wo_ref[sbw:], preferred_element_type=F32))
        h1 = _layer_norm(alpha * h_ref[...] + att, g_ref[...], b_ref[...])
        h1_ref[...] = h1
        logits = jnp.dot(h1.astype(BF16), wr_ref[...], preferred_element_type=F32)
        r_ref[...] = _route(logits, br_ref[...], n_groups, per_group)

    @pl.when(pl.program_id(0) >= n_tiles)
    def _():
        h1_ref[...] = jnp.zeros_like(h1_ref)
        r_ref[...] = jnp.zeros_like(r_ref)


def _merge(osb, od, h, wo, g, b, wr, br, total_rows, tile_offset, prev, alpha, n_groups, per_group):
    n, d = h.shape
    tm = ROW_TILE
    assert n % tm == 0 and total_rows % tm == 0
    n_tiles = n // tm
    grid = n_tiles if prev is not None else total_rows // tm
    row = lambda wd: pl.BlockSpec((tm, wd), lambda i: (jnp.minimum(i, n_tiles - 1), 0))
    const = lambda a: pl.BlockSpec(a.shape, lambda i: (0,) * a.ndim, pipeline_mode=pl.Buffered(1))
    out_row = lambda wd: pl.BlockSpec((tm, wd), lambda i: (i + tile_offset, 0))
    args = [osb, od, h, wo, g, b, wr, br]
    in_specs = [row(osb.shape[1]), row(od.shape[1]), row(d), const(wo), const(g), const(b), const(wr),
                const(br)]
    aliases = {}
    if prev is not None:
        args += list(prev)
        in_specs += [pl.BlockSpec(memory_space=pl.ANY)] * 2
        aliases = {8: 0, 9: 1}
    return pl.pallas_call(
        functools.partial(_merge_kernel, alpha=alpha, n_groups=n_groups, per_group=per_group,
                          aliased=prev is not None, n_tiles=n_tiles),
        grid=(grid,),
        in_specs=in_specs,
        out_specs=(out_row(d), out_row(LANES)),
        out_shape=(jax.ShapeDtypeStruct((total_rows, d), F32),
                   jax.ShapeDtypeStruct((total_rows, LANES), F32)),
        input_output_aliases=aliases,
        compiler_params=_params(("arbitrary",)),
    )(*args)


def _moe_kernel(blk_exp_ref, n_used_ref, n_valid_ref, slot_tok_ref, dst_row_ref, x_hbm, wgu_ref, wd_ref,
                y_hbm, xbuf, ybuf, gsem, ssem):
    i = pl.program_id(0)
    d_exp = wd_ref.shape[1]
    n_used = n_used_ref[0]
    slot = i % 2

    def gather_row(blk, s, r):
        tok = slot_tok_ref[blk * ROW_TILE + r]
        return pltpu.make_async_copy(x_hbm.at[pl.ds(tok, 1)], xbuf.at[s, pl.ds(r, 1)], gsem.at[s])

    def scatter_row(blk, s, r):
        dst = dst_row_ref[blk * ROW_TILE + r]
        d = ybuf.shape[2]
        col = pl.multiple_of((dst & (TOP_K - 1)) * d, d)
        return pltpu.make_async_copy(ybuf.at[s, pl.ds(r, 1)],
                                     y_hbm.at[pl.ds(dst >> TOP_K_SHIFT, 1), pl.ds(col, d)], ssem.at[s])

    def start_rows(copy_row, blk, s):
        def issue(r, carry):
            copy_row(blk, s, r).start()
            return carry
        lax.fori_loop(0, n_valid_ref[blk], issue, 0)

    def wait_rows(copy_row, blk, s):
        n = n_valid_ref[blk]
        n8 = pl.multiple_of((n >> 3) << 3, 8)
        d = ybuf.shape[2]

        @pl.when(n8 > 0)
        def _():
            if copy_row is gather_row:
                pltpu.make_async_copy(x_hbm.at[pl.ds(0, n8)], xbuf.at[s, pl.ds(0, n8)], gsem.at[s]).wait()
            else:
                pltpu.make_async_copy(ybuf.at[s, pl.ds(0, n8)], y_hbm.at[pl.ds(0, n8), pl.ds(0, d)],
                                      ssem.at[s]).wait()

        def done(r, carry):
            copy_row(blk, s, r).wait()
            return carry
        lax.fori_loop(n8, n, done, 0)

    @pl.when(i == 0)
    def _():
        xbuf[...] = jnp.zeros_like(xbuf)
        start_rows(gather_row, 0, 0)

    @pl.when(i < n_used)
    def _():
        @pl.when(i + 1 < n_used)
        def _():
            start_rows(gather_row, i + 1, 1 - slot)

        wait_rows(gather_row, i, slot)

        @pl.when(i >= 2)
        def _():
            wait_rows(scatter_row, i - 2, slot)

        x = xbuf[slot].astype(BF16)
        gu = jnp.dot(x, wgu_ref[0], preferred_element_type=F32)
        gate, up = gu[:, :d_exp], gu[:, d_exp:]
        hmid = (gate * jax.nn.sigmoid(gate) * up).astype(BF16)
        ybuf[slot] = jnp.dot(hmid, wd_ref[0], preferred_element_type=F32)
        start_rows(scatter_row, i, slot)

    @pl.when(i == pl.num_programs(0) - 1)
    def _():
        wait_rows(scatter_row, n_used - 1, (n_used - 1) % 2)

        @pl.when(n_used >= 2)
        def _():
            wait_rows(scatter_row, n_used - 2, n_used % 2)


def _moe(blk_exp, n_used, n_valid, slot_tok, dst_row, x_all, wgu16, wd16):
    n_blocks = blk_exp.shape[0]
    tm = ROW_TILE
    n_tok, d = x_all.shape
    grid_spec = pltpu.PrefetchScalarGridSpec(
        num_scalar_prefetch=5,
        grid=(n_blocks,),
        in_specs=[pl.BlockSpec(memory_space=pl.ANY),
                  pl.BlockSpec((1,) + wgu16.shape[1:], lambda i, be, *_: (be[i], 0, 0)),
                  pl.BlockSpec((1,) + wd16.shape[1:], lambda i, be, *_: (be[i], 0, 0))],
        out_specs=pl.BlockSpec(memory_space=pl.ANY),
        scratch_shapes=[pltpu.VMEM((2, tm, d), F32), pltpu.VMEM((2, tm, d), F32),
                        pltpu.SemaphoreType.DMA((2,)), pltpu.SemaphoreType.DMA((2,))],
    )
    return pl.pallas_call(
        _moe_kernel,
        grid_spec=grid_spec,
        out_shape=jax.ShapeDtypeStruct((n_tok, TOP_K * d), F32),
        compiler_params=_params(("arbitrary",)),
    )(blk_exp, n_used, n_valid, slot_tok, dst_row, x_all, wgu16, wd16)


def _dispatch_tables(expert_idx, n_experts, tm):
    n_tok = expert_idx.shape[0]
    n_assign = n_tok * TOP_K
    flat_e = expert_idx.reshape(-1)
    order = jnp.argsort(flat_e).astype(I32)
    sorted_e = flat_e[order]
    counts = jnp.bincount(flat_e, length=n_experts).astype(I32)
    padded = (counts + tm - 1) // tm * tm
    pad_end = jnp.cumsum(padded)
    pad_start = pad_end - padded
    start = jnp.cumsum(counts) - counts
    slot = pad_start[sorted_e] + jnp.arange(n_assign, dtype=I32) - start[sorted_e]
    n_blocks = -(-n_assign // tm) + n_experts
    n_slots = n_blocks * tm
    slot_tok = jnp.zeros((n_slots,), I32).at[slot].set(order // TOP_K)
    dst_row = jnp.zeros((n_slots,), I32).at[slot].set(order)
    n_valid = jnp.zeros((n_slots,), I32).at[slot].set(1).reshape(n_blocks, tm).sum(axis=1)
    n_used = (pad_end[-1] // tm).astype(I32)
    blk = jnp.minimum(jnp.arange(n_blocks, dtype=I32), n_used - 1)
    blk_exp = jnp.minimum(jnp.searchsorted(pad_end, blk * tm, side='right'), n_experts - 1).astype(I32)
    return blk_exp, n_used.reshape(1), n_valid, slot_tok, dst_row


def _combine_kernel(y_ref, r_ref, h_ref, g_ref, b_ref, o_ref, *, alpha):
    d = h_ref.shape[1]
    r = r_ref[...]
    ffn = y_ref[:, :d] * r[:, 2:3] + y_ref[:, d:] * r[:, 3:4]
    o_ref[...] = _layer_norm(alpha * h_ref[...] + ffn, g_ref[...], b_ref[...])


def _combine(y_pairs, r_all, h_all, g, b, alpha):
    n, d = h_all.shape
    tm = ROW_TILE
    row = lambda wd: pl.BlockSpec((tm, wd), lambda i: (i, 0))
    const = lambda a: pl.BlockSpec(a.shape, lambda i: (0,) * a.ndim)
    return pl.pallas_call(
        functools.partial(_combine_kernel, alpha=alpha),
        grid=(n // tm,),
        in_specs=[row(2 * d), row(LANES), row(d), const(g), const(b)],
        out_specs=row(d),
        out_shape=jax.ShapeDtypeStruct((n, d), F32),
        compiler_params=_params(("arbitrary",)),
    )(y_pairs, r_all, h_all, g, b)


def kernel(x_prompt, x_sample, cache_k_sb, cache_v_sb, cache_k_diff, cache_v_diff, page_table, meta_tokens, ln_in_g, ln_in_b, rel_bias, w_in, sb_norm_g, diff_lambda, diff_subln_g, w_out, ln1_g, ln1_b, w_group, b_group, w_expert_router, b_expert_router, w_gate_up, w_down, ln2_g, ln2_b):
    bsz, seq, d = x_prompt.shape
    db = x_sample.shape[0]
    assert x_sample.shape[1] == 1
    depth = w_in.shape[0]
    assert depth == 1, "the prompt/decode buffers below are laid out for a single layer"
    n_meta = meta_tokens.shape[0]
    hd = sb_norm_g.shape[-1]
    sb_kv, diff_kv = cache_k_sb.shape[3], cache_k_diff.shape[3]
    sb_w, diff_w = d // 2, d // 2
    sb_kv_w, diff_kv_w = sb_kv * hd, diff_kv * 2 * hd
    splits = [0, sb_w, sb_w + sb_kv_w, sb_w + 2 * sb_kv_w, sb_w + 2 * sb_kv_w + diff_w,
              sb_w + 2 * sb_kv_w + diff_w + diff_kv_w, sb_w + 2 * sb_kv_w + diff_w + 2 * diff_kv_w]
    assert splits[-1] == w_in.shape[-1]
    n_groups = w_group.shape[-1]
    n_experts = w_expert_router.shape[-1]
    per_group = n_experts // n_groups
    page = cache_k_sb.shape[2]
    past = page_table.shape[1] * page
    alpha = (2 * depth) ** 0.25
    li = 0
    lam_init = 0.8 - 0.6 * math.exp(-0.3 * li)

    seq_len = n_meta + seq
    lp = -(-seq_len // (LANES * PROJ_TILE // math.gcd(LANES, PROJ_TILE))) * (LANES * PROJ_TILE // math.gcd(LANES, PROJ_TILE))
    xpad = jnp.concatenate([jnp.broadcast_to(meta_tokens[None], (bsz, n_meta, d)), x_prompt,
                            jnp.zeros((bsz, lp - seq_len, d), F32)], axis=1)
    g_in, b_in = ln_in_g.reshape(1, d), ln_in_b.reshape(1, d)
    w_in16 = w_in[li].astype(BF16)
    w_out16 = w_out[li].astype(BF16)
    wgu16 = w_gate_up[li].astype(BF16)
    wd16 = w_down[li].astype(BF16)
    w_router = jnp.zeros((d, LANES), F32).at[:, :n_groups].set(w_group[li]) \
        .at[:, n_groups:n_groups + n_experts].set(w_expert_router[li])
    b_router = jnp.zeros((1, LANES), F32).at[0, :n_groups].set(b_group[li]) \
        .at[0, n_groups:n_groups + n_experts].set(b_expert_router[li])
    w_router = w_router.astype(BF16)
    g1, b1 = ln1_g[li].reshape(1, d), ln1_b[li].reshape(1, d)
    g2, b2 = ln2_g[li].reshape(1, d), ln2_b[li].reshape(1, d)
    g_sb = sb_norm_g[li].reshape(1, hd)
    g_diff = diff_subln_g[li].reshape(1, 2 * hd)
    lam_params = diff_lambda[li]
    idx = jnp.arange(MXU_DIM)
    tri_half = (idx[:LANES, None] >= idx[None, :LANES])
    tri = jnp.concatenate([tri_half, jnp.ones((LANES, LANES), bool)], axis=1).astype(BF16)
    tri = jnp.concatenate([tri, tri], axis=0)

    near, far, dec = _bias_tables(rel_bias, past)

    (hp, qsb16, qd16, ksb16, vsb16, kd16, vd16, k_sb, v_sb, k_d, v_d) = _ln_proj_prompt(
        xpad, g_in, b_in, w_in16, splits, seq_len)
    o_sb = _sb_prompt(qsb16, ksb16, vsb16, tri, g_sb, sb_kv)
    o_d = _diff_prompt(qd16, kd16, vd16, near, far, lam_params, g_diff, diff_kv, lam_init)
    n_prompt = bsz * lp
    total_rows = n_prompt + ROW_TILE
    assert n_prompt % ROW_TILE == 0 and db <= ROW_TILE
    h1_all, r_all = _merge(o_sb.reshape(n_prompt, sb_w), o_d.reshape(n_prompt, diff_w),
                           hp.reshape(n_prompt, d), w_out16, g1, b1, w_router, b_router, total_rows, 0,
                           None, alpha, n_groups, per_group)

    hs, proj_s = _ln_proj_rows(x_sample.reshape(db, d), g_in, b_in, w_in16)
    qn_sb, kn_sb, vn_sb, qn_d, kn_d, vn_d = [proj_s[:, splits[t]:splits[t + 1]] for t in range(6)]
    pt_flat = page_table.reshape(-1).astype(I32)
    os_sb = _sb_decode(pt_flat, qn_sb, tri, g_sb, cache_k_sb, cache_v_sb, li)
    dec_heads = dec[:, 0, :].reshape(diff_kv, -1, past)
    dec_rows = jnp.concatenate([dec_heads] * 2, axis=1)
    bias0 = jnp.concatenate([(rel_bias[0] * LOG2E).reshape(diff_kv, -1, 1)] * 2, axis=1)
    os_d = _diff_decode(pt_flat, qn_d, kn_d, vn_d, dec_rows, bias0, lam_params, g_diff, cache_k_diff,
                        cache_v_diff, li, lam_init)
    pad_rows = lambda a: jnp.concatenate([a, jnp.zeros((ROW_TILE - db, a.shape[1]), a.dtype)], axis=0)
    h1_all, r_all = _merge(pad_rows(os_sb), pad_rows(os_d), pad_rows(hs), w_out16, g1, b1, w_router,
                           b_router, total_rows, n_prompt // ROW_TILE, (h1_all, r_all), alpha, n_groups,
                           per_group)

    expert_idx = r_all[:, :TOP_K].astype(I32)
    tables = _dispatch_tables(expert_idx, n_experts, ROW_TILE)
    y_pairs = _moe(*tables, h1_all, wgu16, wd16)
    y_all = _combine(y_pairs, r_all, h1_all, g2, b2, alpha)

    y_prompt = y_all[:n_prompt].reshape(bsz, lp, d)[:, n_meta:seq_len]
    y_sample = y_all[n_prompt:n_prompt + db].reshape(db, 1, d)
    return (y_prompt, y_sample,
            k_sb.reshape(1, bsz, seq_len, sb_kv, hd), v_sb.reshape(1, bsz, seq_len, sb_kv, hd),
            k_d.reshape(1, bsz, seq_len, diff_kv, 2 * hd), v_d.reshape(1, bsz, seq_len, diff_kv, 2 * hd),
            kn_sb.reshape(1, db, 1, sb_kv, hd), vn_sb.reshape(1, db, 1, sb_kv, hd),
            kn_d.reshape(1, db, 1, diff_kv, 2 * hd), vn_d.reshape(1, db, 1, diff_kv, 2 * hd))
```

```python
import functools
import math

import jax
import jax.numpy as jnp
from jax import lax
from jax.experimental import pallas as pl
from jax.experimental.pallas import tpu as pltpu

F32 = jnp.float32
BF16 = jnp.bfloat16
I32 = jnp.int32

LANES = 128
MXU_DIM = 256
LN_EPS = 1e-5
REL_BUCKETS, REL_MAX_EXACT, REL_MAX_DIST = 32, 16, 128
TOP_K = 2
TOP_K_SHIFT = TOP_K.bit_length() - 1
LOG2E = math.log2(math.e)
MASKED = -1e30
SB_UNDERFLOW = -104.0
ROW_TILE = 256
PROJ_TILE = 320
DEC_PAGES = 16
FAR_TILES = 8
VMEM_LIMIT = 56 * 1024 * 1024

NT_DIMS = (((1,), (1,)), ((), ()))


def _params(sem, vmem=VMEM_LIMIT):
    return pltpu.CompilerParams(dimension_semantics=sem, vmem_limit_bytes=vmem)


def _layer_norm(x, g, b):
    mu = jnp.mean(x, axis=-1, keepdims=True)
    xc = x - mu
    var = jnp.mean(xc * xc, axis=-1, keepdims=True)
    return xc * lax.rsqrt(var + LN_EPS) * g + b


def _rms_norm(x, g):
    return x * lax.rsqrt(jnp.mean(x * x, axis=-1, keepdims=True) + LN_EPS) * g


def _split_bf16(x):
    hi = x.astype(BF16)
    lo = (x - hi.astype(F32)).astype(BF16)
    return hi, lo


def _lambda_value(lp, lam_init):
    a = jnp.sum(lp[0:1] * lp[1:2], axis=-1, keepdims=True)
    b = jnp.sum(lp[2:3] * lp[3:4], axis=-1, keepdims=True)
    return jnp.exp(a) - jnp.exp(b) + lam_init


def _ln_proj_prompt_kernel(x_ref, g_ref, b_ref, w_ref, h_ref, qsb_ref, qd_ref,
                           ksb16_ref, vsb16_ref, kd16_ref, vd16_ref,
                           ksb_ref, vsb_ref, kd_ref, vd_ref, *, splits):
    h = _layer_norm(x_ref[0], g_ref[...], b_ref[...])
    h_ref[0] = h
    hb = h.astype(BF16)

    def proj(seg):
        lo, hi = splits[seg], splits[seg + 1]
        return jnp.dot(hb, w_ref[:, lo:hi], preferred_element_type=F32)

    def emit(seg, full_ref, half_ref):
        y = proj(seg)
        width = full_ref.shape[3]
        for h in range(full_ref.shape[2]):
            full_ref[0, :, h, :] = y[:, h * width:(h + 1) * width]
        half_ref[0] = y.astype(BF16)

    qsb_ref[0] = proj(0).astype(BF16)
    emit(1, ksb_ref, ksb16_ref)
    emit(2, vsb_ref, vsb16_ref)
    qd_ref[0] = proj(3).astype(BF16)
    emit(4, kd_ref, kd16_ref)
    emit(5, vd_ref, vd16_ref)


def _ln_proj_prompt(xpad, g, b, w16, splits, seq_len, sb_kv, diff_kv):
    bsz, lp, d = xpad.shape
    widths = [splits[i + 1] - splits[i] for i in range(6)]
    tm = PROJ_TILE
    assert lp % tm == 0 and lp - seq_len < tm
    row = lambda wd: pl.BlockSpec((1, tm, wd), lambda bi, i: (bi, i, 0))
    heads = lambda n, wd: pl.BlockSpec((1, tm, n, wd // n), lambda bi, i: (bi, i, 0, 0))
    const = lambda shp: pl.BlockSpec(shp, lambda bi, i: (0,) * len(shp))
    out_shape = (
        jax.ShapeDtypeStruct((bsz, lp, d), F32),
        jax.ShapeDtypeStruct((bsz, lp, widths[0]), BF16),
        jax.ShapeDtypeStruct((bsz, lp, widths[3]), BF16),
        jax.ShapeDtypeStruct((bsz, lp, widths[1]), BF16),
        jax.ShapeDtypeStruct((bsz, lp, widths[2]), BF16),
        jax.ShapeDtypeStruct((bsz, lp, widths[4]), BF16),
        jax.ShapeDtypeStruct((bsz, lp, widths[5]), BF16),
        jax.ShapeDtypeStruct((bsz, seq_len, sb_kv, widths[1] // sb_kv), F32),
        jax.ShapeDtypeStruct((bsz, seq_len, sb_kv, widths[2] // sb_kv), F32),
        jax.ShapeDtypeStruct((bsz, seq_len, diff_kv, widths[4] // diff_kv), F32),
        jax.ShapeDtypeStruct((bsz, seq_len, diff_kv, widths[5] // diff_kv), F32),
    )
    out_specs = (row(d), row(widths[0]), row(widths[3]), row(widths[1]), row(widths[2]),
                 row(widths[4]), row(widths[5]), heads(sb_kv, widths[1]), heads(sb_kv, widths[2]),
                 heads(diff_kv, widths[4]), heads(diff_kv, widths[5]))
    return pl.pallas_call(
        functools.partial(_ln_proj_prompt_kernel, splits=tuple(splits)),
        grid=(bsz, lp // tm),
        in_specs=[row(d), const((1, d)), const((1, d)),
                  pl.BlockSpec(w16.shape, lambda bi, i: (0, 0), pipeline_mode=pl.Buffered(1))],
        out_specs=out_specs,
        out_shape=out_shape,
        compiler_params=_params(("arbitrary", "arbitrary")),
    )(xpad, g, b, w16)


def _ln_proj_rows_kernel(x_ref, g_ref, b_ref, w_ref, h_ref, p_ref):
    h = _layer_norm(x_ref[...], g_ref[...], b_ref[...])
    h_ref[...] = h
    p_ref[...] = jnp.dot(h.astype(BF16), w_ref[...], preferred_element_type=F32)


def _ln_proj_rows(x, g, b, w):
    n, d = x.shape
    pw = w.shape[1]
    tn = 512
    assert pw % tn == 0
    return pl.pallas_call(
        _ln_proj_rows_kernel,
        grid=(pw // tn,),
        in_specs=[pl.BlockSpec((n, d), lambda j: (0, 0)), pl.BlockSpec((1, d), lambda j: (0, 0)),
                  pl.BlockSpec((1, d), lambda j: (0, 0)), pl.BlockSpec((d, tn), lambda j: (0, j))],
        out_specs=(pl.BlockSpec((n, d), lambda j: (0, 0)), pl.BlockSpec((n, tn), lambda j: (0, j))),
        out_shape=(jax.ShapeDtypeStruct((n, d), F32), jax.ShapeDtypeStruct((n, pw), F32)),
        compiler_params=_params(("arbitrary",)),
    )(x, g, b, w)


def _bias_from_dist(dist, rel_ref, head):
    n = jnp.maximum(dist, 0)
    nf = jnp.maximum(n, 1).astype(F32)
    large = REL_MAX_EXACT + (jnp.log(nf / REL_MAX_EXACT) / math.log(REL_MAX_DIST / REL_MAX_EXACT)
                             * (REL_BUCKETS - REL_MAX_EXACT)).astype(I32)
    large = jnp.minimum(large, REL_BUCKETS - 1)
    bucket = jnp.where(n < REL_MAX_EXACT, n, large)
    out = jnp.zeros(dist.shape, F32)
    for bkt in range(REL_BUCKETS):
        out = jnp.where(bucket == bkt, rel_ref[bkt, head], out)
    return out * LOG2E


def _bias_tables_kernel(rel_ref, near_ref, far_ref, dec_ref, *, n_heads, past):
    i = lax.broadcasted_iota(I32, (LANES, 2 * LANES), 0)
    j = lax.broadcasted_iota(I32, (LANES, 2 * LANES), 1)
    dist = i + LANES - j
    kpos = lax.broadcasted_iota(I32, dec_ref.shape[1:], 1)
    for h in range(n_heads):
        near_ref[h] = jnp.where(dist >= 0, _bias_from_dist(dist, rel_ref, h), MASKED)
        far_ref[h] = jnp.full(far_ref.shape[1:], rel_ref[REL_BUCKETS - 1, h] * LOG2E, F32)
        dec_ref[h] = _bias_from_dist(past - kpos, rel_ref, h)


def _bias_tables(rel_bias, past):
    n_heads = rel_bias.shape[1]
    return pl.pallas_call(
        functools.partial(_bias_tables_kernel, n_heads=n_heads, past=past),
        in_specs=[pl.BlockSpec(memory_space=pltpu.SMEM)],
        out_shape=(jax.ShapeDtypeStruct((n_heads, LANES, 2 * LANES), F32),
                   jax.ShapeDtypeStruct((n_heads, LANES, LANES), F32),
                   jax.ShapeDtypeStruct((n_heads, 8, past), F32)),
    )(rel_bias)


def _sb_tile(z, vis, tri, carry):
    l = -(jnp.maximum(z, 0.0) + jnp.log(1.0 + jnp.exp(-jnp.abs(z))))
    if vis is not None:
        l = jnp.where(vis, l, 0.0)
    hi, lo = _split_bf16(l)
    t2 = jnp.dot(jnp.concatenate([hi, lo], axis=1), tri, preferred_element_type=F32)
    nt = z.shape[1]
    incl, total = t2[:, :nt], t2[:, nt:]
    w = jnp.exp(z + incl + carry)
    if vis is not None:
        w = jnp.where(vis, w, 0.0)
    return w, carry + total


def _sb_prompt_kernel(q_ref, k_ref, v_ref, tri_ref, g_ref, o_ref, acc_ref, carry_ref, *, scale):
    qi = pl.program_id(1)
    n_kv, m, hd = acc_ref.shape
    group = m // LANES
    q = q_ref[0]
    qh = [jnp.concatenate([q[:, (h * group + g) * hd:(h * group + g + 1) * hd] for g in range(group)], axis=0)
          for h in range(n_kv)]
    acc_ref[...] = jnp.zeros_like(acc_ref)
    carry_ref[...] = jnp.zeros_like(carry_ref)
    rowpos = lax.broadcasted_iota(I32, (m, LANES), 0) & (LANES - 1)
    col = lax.broadcasted_iota(I32, (m, LANES), 1)
    tri = tri_ref[...]

    def cond(state):
        j, live = state
        return jnp.logical_and(j >= 0, live > SB_UNDERFLOW)

    def body(state):
        j, _ = state
        ks = pl.multiple_of(j * LANES, LANES)
        vis = col < rowpos + (qi - j) * LANES
        live = None
        for h in range(n_kv):
            k = k_ref[0, pl.ds(ks, LANES), h * hd:(h + 1) * hd]
            v = v_ref[0, pl.ds(ks, LANES), h * hd:(h + 1) * hd]
            z = lax.dot_general(qh[h], k, NT_DIMS, preferred_element_type=F32) * scale
            w, carry = _sb_tile(z, vis, tri, carry_ref[h])
            acc_ref[h] += jnp.dot(w.astype(BF16), v, preferred_element_type=F32)
            carry_ref[h] = carry
            live = carry if live is None else jnp.maximum(live, carry)
        return j - 1, jnp.max(live)

    lax.while_loop(cond, body, (qi, jnp.float32(0.0)))
    outs = []
    for h in range(n_kv):
        o = _rms_norm(acc_ref[h], g_ref[...]).astype(o_ref.dtype)
        outs += [o[g * LANES:(g + 1) * LANES] for g in range(group)]
    o_ref[0] = jnp.concatenate(outs, axis=1)


def _sb_prompt(q16, k16, v16, tri, g, n_kv):
    bsz, lp, width = q16.shape
    kvw = k16.shape[2]
    hd = kvw // n_kv
    m = width // n_kv // hd * LANES
    resident = lambda: pl.BlockSpec((1, lp, kvw), lambda b, i: (b, 0, 0), pipeline_mode=pl.Buffered(1))
    return pl.pallas_call(
        functools.partial(_sb_prompt_kernel, scale=hd ** -0.5),
        grid=(bsz, lp // LANES),
        in_specs=[pl.BlockSpec((1, LANES, width), lambda b, i: (b, i, 0)),
                  resident(), resident(),
                  pl.BlockSpec(tri.shape, lambda b, i: (0, 0)),
                  pl.BlockSpec((1, hd), lambda b, i: (0, 0))],
        out_specs=pl.BlockSpec((1, LANES, width), lambda b, i: (b, i, 0)),
        out_shape=jax.ShapeDtypeStruct((bsz, lp, width), BF16),
        scratch_shapes=[pltpu.VMEM((n_kv, m, hd), F32), pltpu.VMEM((n_kv, m, LANES), F32)],
        compiler_params=_params(("arbitrary",) * 2),
    )(q16, k16, v16, tri, g)


def _sb_decode_kernel(pt_ref, q_ref, tri_ref, g_ref, k_hbm, v_hbm, o_ref, kbuf, vbuf, sem, *,
                      scale, n_kv, n_pages, layer):
    s = pl.program_id(0)
    q = q_ref[0]
    hd = g_ref.shape[1]
    n_heads = q.shape[1] // hd
    group = n_heads // n_kv
    zeros = jnp.zeros((1, hd), F32)
    rows = []
    for n in range(n_heads):
        seg = q[:, n * hd:(n + 1) * hd]
        rows.append(jnp.concatenate([seg if h == n // group else zeros for h in range(n_kv)], axis=1))
    qb = jnp.concatenate(rows, axis=0)
    q16 = qb.astype(BF16)
    tri = tri_ref[...]

    def page_copies(j, slot):
        page = pt_ref[s * n_pages + j]
        return [pltpu.make_async_copy(hbm.at[layer, page, :, h, :], buf.at[slot, h], sem.at[which, slot])
                for which, (hbm, buf) in enumerate(((k_hbm, kbuf), (v_hbm, vbuf))) for h in range(n_kv)]

    def fetch(j, slot):
        for cp in page_copies(j, slot):
            cp.start()

    def wait_page(j, slot):
        for cp in page_copies(j, slot):
            cp.wait()

    def heads_on_lanes(buf, slot):
        return jnp.concatenate([buf[slot, h] for h in range(n_kv)], axis=1)

    fetch(n_pages - 1, 0)

    def cond(state):
        j, live, _, _ = state
        return jnp.logical_and(j >= 0, live > SB_UNDERFLOW)

    def body(state):
        j, _, carry, acc = state
        slot = (n_pages - 1 - j) % 2
        wait_page(j, slot)

        @pl.when(j > 0)
        def _():
            fetch(j - 1, 1 - slot)

        k = heads_on_lanes(kbuf, slot).astype(BF16)
        v = heads_on_lanes(vbuf, slot).astype(BF16)
        z = lax.dot_general(q16, k, NT_DIMS, preferred_element_type=F32) * scale
        w, carry = _sb_tile(z, None, tri, carry)
        acc = acc + jnp.dot(w.astype(BF16), v, preferred_element_type=F32)
        return j - 1, jnp.max(carry), carry, acc

    init = (jnp.int32(n_pages - 1), jnp.float32(0.0), jnp.zeros((n_heads, LANES), F32),
            jnp.zeros((n_heads, n_kv * hd), F32))
    j_end, _, _, acc = lax.while_loop(cond, body, init)

    @pl.when(j_end >= 0)
    def _():
        wait_page(j_end, (n_pages - 1 - j_end) % 2)

    outs = []
    for n in range(n_heads):
        h = n // group
        outs.append(_rms_norm(acc[n:n + 1, h * hd:(h + 1) * hd], g_ref[...]))
    o_ref[0] = jnp.concatenate(outs, axis=1)


def _sb_decode(pt_flat, q, tri, g, cache_k, cache_v, layer):
    db, width = q.shape
    depth, n_pool, ps, n_kv, hd = cache_k.shape
    assert ps == LANES
    n_pages = pt_flat.shape[0] // db
    grid_spec = pltpu.PrefetchScalarGridSpec(
        num_scalar_prefetch=1,
        grid=(db,),
        in_specs=[pl.BlockSpec((1, 1, width), lambda s, pt: (s, 0, 0)),
                  pl.BlockSpec(tri.shape, lambda s, pt: (0, 0)),
                  pl.BlockSpec((1, hd), lambda s, pt: (0, 0)),
                  pl.BlockSpec(memory_space=pl.ANY), pl.BlockSpec(memory_space=pl.ANY)],
        out_specs=pl.BlockSpec((1, 1, width), lambda s, pt: (s, 0, 0)),
        scratch_shapes=[pltpu.VMEM((2, n_kv, ps, hd), F32), pltpu.VMEM((2, n_kv, ps, hd), F32),
                        pltpu.SemaphoreType.DMA((2, 2))],
    )
    out = pl.pallas_call(
        functools.partial(_sb_decode_kernel, scale=hd ** -0.5, n_kv=n_kv, n_pages=n_pages, layer=layer),
        grid_spec=grid_spec,
        out_shape=jax.ShapeDtypeStruct((db, 1, width), F32),
        compiler_params=_params(("arbitrary",)),
    )(pt_flat, q.reshape(db, 1, width), tri, g, cache_k, cache_v)
    return out.reshape(db, width)


def _diff_prompt_kernel(q_ref, k_ref, v_ref, near_ref, far_ref, lam_ref, g_ref, o_ref,
                        m_ref, l_ref, acc_ref, *, scale, lam_init, far_tiles):
    qi = pl.program_id(2)
    q = q_ref[0]
    hd = k_ref.shape[2] // 2
    group = q.shape[1] // (2 * hd)
    rows = group * LANES
    qm = [jnp.concatenate([q[:, (2 * g + c) * hd:(2 * g + c + 1) * hd] for g in range(group)], axis=0)
          for c in range(2)]
    c1 = scale * LOG2E

    def scores(c, ks, width):
        k = k_ref[0, pl.ds(ks, width), c * hd:(c + 1) * hd]
        return lax.dot_general(qm[c], k, NT_DIMS, preferred_element_type=F32) * c1

    def lane_tiles(x):
        return [x[:, t * LANES:(t + 1) * LANES] for t in range(x.shape[1] // LANES)]

    def row_max(s):
        return jnp.max(functools.reduce(jnp.maximum, lane_tiles(s)), axis=-1, keepdims=True)

    def exp_tiles(s, shift):
        ps = [jnp.exp2(t - shift) for t in lane_tiles(s)]
        return jnp.concatenate(ps, axis=1).astype(BF16), functools.reduce(jnp.add, ps)

    def near_chunk(ks, bias, width):
        v = v_ref[0, pl.ds(ks, width), :]
        for c in range(2):
            s = scores(c, ks, width) + bias
            mx = jnp.broadcast_to(row_max(s), (rows, LANES))
            p, psum = exp_tiles(s, mx)
            m_ref[0, c] = mx
            l_ref[0, c] = psum
            acc_ref[0, c] = jnp.dot(p, v, preferred_element_type=F32)

    @pl.when(qi == 0)
    def _():
        near_chunk(0, near_ref[0, :, LANES:], LANES)

    @pl.when(qi > 0)
    def _():
        near_chunk(pl.multiple_of((qi - 1) * LANES, LANES), near_ref[0], 2 * LANES)

    m_ref[1] = jnp.full(m_ref.shape[1:], MASKED, F32)
    l_ref[1] = jnp.zeros(l_ref.shape[1:], F32)
    acc_ref[1] = jnp.zeros(acc_ref.shape[1:], F32)

    def far_chunk(ks, stream, key_range=None):
        v = v_ref[0, pl.ds(ks, wide), :]
        fb = far_ref[0]
        for c in range(2):
            s = scores(c, ks, wide)
            if key_range is not None:
                kpos = ks + lax.broadcasted_iota(I32, s.shape, 1)
                s = jnp.where(jnp.logical_and(kpos >= key_range[0], kpos < key_range[1]), s, MASKED)
            m_old = m_ref[stream, c]
            m_new = jnp.maximum(m_old, row_max(s) + fb)
            p, psum = exp_tiles(s, m_new - fb)
            pv = jnp.dot(p, v, preferred_element_type=F32)
            alpha = jnp.exp2(m_old - m_new)
            l_ref[stream, c] = alpha * l_ref[stream, c] + psum
            acc_ref[stream, c] = jnp.concatenate([alpha] * (pv.shape[1] // LANES), axis=1) * acc_ref[stream, c] + pv
            m_ref[stream, c] = m_new

    n_far = jnp.maximum(qi - 1, 0)
    wide = far_tiles * LANES
    n_full = n_far // far_tiles

    def pair_body(t, carry):
        far_chunk(pl.multiple_of(2 * t * wide, wide), 0)
        far_chunk(pl.multiple_of((2 * t + 1) * wide, wide), 1)
        return carry

    lax.fori_loop(0, n_full // 2, pair_body, 0)

    @pl.when(n_full % 2 == 1)
    def _():
        far_chunk(pl.multiple_of((n_full - 1) * wide, wide), 0)

    @pl.when(n_far % far_tiles != 0)
    def _():
        ks = pl.multiple_of(jnp.minimum(n_full * wide, k_ref.shape[1] - wide), LANES)
        far_chunk(ks, 1, (n_full * wide, n_far * LANES))

    lam = _lambda_value(lam_ref[...], lam_init)
    merged = []
    for c in range(2):
        m = jnp.maximum(m_ref[0, c], m_ref[1, c])
        a0, a1 = jnp.exp2(m_ref[0, c] - m), jnp.exp2(m_ref[1, c] - m)
        l = jnp.sum(a0 * l_ref[0, c] + a1 * l_ref[1, c], axis=-1, keepdims=True)
        merged.append((a0[:, :1] * acc_ref[0, c] + a1[:, :1] * acc_ref[1, c]) / l)
    a = merged[0] - lam * merged[1]
    o = _rms_norm(a, g_ref[...] * (1.0 - lam_init)).astype(o_ref.dtype)
    o_ref[0] = jnp.concatenate([o[g * LANES:(g + 1) * LANES] for g in range(group)], axis=1)


def _diff_prompt(q16, k16, v16, near, far, lam_params, g, n_kv, lam_init):
    bsz, lp, width = q16.shape
    gw = width // n_kv
    kw = k16.shape[2] // n_kv
    hd = kw // 2
    group = gw // kw
    rows = group * LANES
    nq = lp // LANES
    far_tiles = min(FAR_TILES, nq)
    near_g = near.reshape(n_kv, rows, 2 * LANES)
    far_g = far.reshape(n_kv, rows, LANES)
    return pl.pallas_call(
        functools.partial(_diff_prompt_kernel, scale=hd ** -0.5, lam_init=lam_init, far_tiles=far_tiles),
        grid=(bsz, n_kv, nq),
        in_specs=[pl.BlockSpec((1, LANES, gw), lambda b, h, i: (b, i, h)),
                  pl.BlockSpec((1, lp, kw), lambda b, h, i: (b, 0, h)),
                  pl.BlockSpec((1, lp, kw), lambda b, h, i: (b, 0, h)),
                  pl.BlockSpec((1, rows, 2 * LANES), lambda b, h, i: (h, 0, 0)),
                  pl.BlockSpec((1, rows, LANES), lambda b, h, i: (h, 0, 0)),
                  pl.BlockSpec(lam_params.shape, lambda b, h, i: (0, 0)),
                  pl.BlockSpec((1, kw), lambda b, h, i: (0, 0))],
        out_specs=pl.BlockSpec((1, LANES, gw), lambda b, h, i: (b, i, h)),
        out_shape=jax.ShapeDtypeStruct((bsz, lp, width), BF16),
        scratch_shapes=[pltpu.VMEM((2, 2, rows, LANES), F32), pltpu.VMEM((2, 2, rows, LANES), F32),
                        pltpu.VMEM((2, 2, rows, kw), F32)],
        compiler_params=_params(("arbitrary",) * 3),
    )(q16, k16, v16, near_g, far_g, lam_params, g)


def _diff_decode_kernel(pt_ref, q_ref, kn_ref, vn_ref, bias_ref, bias0_ref, lam_ref, g_ref, k_hbm, v_hbm,
                        o_ref, kbuf, vbuf, sem, s_ref, snew_ref, m_ref, a_ref, anew_ref, acc_ref, *,
                        scale, lam_init, n_pages, pages, layer, n_seq):
    s = pl.program_id(0)
    c = pl.program_id(1)
    nc = n_pages // pages
    n_kv, span, kw = kbuf.shape[1:]
    ps = span // pages
    hd = kw // 2
    group = q_ref.shape[2] // (n_kv * kw)
    step = s * nc + c
    slot = step % 2
    c1 = scale * LOG2E

    def chunk_copies(step_idx, dst_slot):
        seq, ch = step_idx // nc, step_idx % nc
        base_k = jnp.minimum(seq, n_seq - 1) * n_pages + ch * pages
        base_v = jnp.maximum(seq - 1, 0) * n_pages + ch * pages
        out = []
        for t in range(pages):
            for which, (hbm, buf, base) in enumerate(((k_hbm, kbuf, base_k), (v_hbm, vbuf, base_v))):
                page = pt_ref[base + t]
                out += [pltpu.make_async_copy(hbm.at[layer, page, :, h, :],
                                              buf.at[dst_slot, h, pl.ds(t * ps, ps)], sem.at[which, dst_slot])
                        for h in range(n_kv)]
        return out

    @pl.when(step == 0)
    def _():
        for cp in chunk_copies(step, slot):
            cp.start()

    @pl.when(step + 1 < (n_seq + 1) * nc)
    def _():
        for cp in chunk_copies(step + 1, 1 - slot):
            cp.start()

    for cp in chunk_copies(step, slot):
        cp.wait()

    lam = _lambda_value(lam_ref[...], lam_init)
    cols = pl.ds(pl.multiple_of(c * span, span), span)

    @pl.when(s < n_seq)
    def _():
        q = q_ref[0]
        zeros = jnp.zeros((1, hd), F32)
        for h in range(n_kv):
            rows = []
            for mp in range(2):
                for g in range(group):
                    lo = ((h * group + g) * 2 + mp) * hd
                    rows.append(jnp.concatenate([q[:, lo:lo + hd], zeros] if mp == 0 else
                                                [zeros, q[:, lo:lo + hd]], axis=1))
            qb = jnp.concatenate(rows, axis=0).astype(BF16)

            @pl.when(c == 0)
            def _():
                kn = kn_ref[0][:, h * kw:(h + 1) * kw].astype(BF16)
                s0 = jnp.sum(qb.astype(F32) * kn.astype(F32), axis=-1, keepdims=True) * c1 + bias0_ref[h]
                snew_ref[h] = jnp.broadcast_to(s0, snew_ref.shape[1:])
                m_ref[h] = jnp.broadcast_to(s0, m_ref.shape[1:])

            sc = lax.dot_general(qb, kbuf[slot, h].astype(BF16), NT_DIMS,
                                 preferred_element_type=F32) * c1 + bias_ref[h]
            s_ref[h, :, cols] = sc
            m_ref[h] = jnp.maximum(m_ref[h], jnp.max(sc, axis=-1, keepdims=True))

            @pl.when(c == nc - 1)
            def _():
                m = m_ref[h][:, :1]
                p = jnp.exp2(s_ref[h] - m)
                p0 = jnp.exp2(snew_ref[h][:, :1] - m)
                l = jnp.sum(p, axis=-1, keepdims=True) + p0
                pn, pn0 = p / l, p0 / l
                a_ref[s % 2, h] = (pn[:group] - lam * pn[group:]).astype(a_ref.dtype)
                anew_ref[s % 2, h] = jnp.broadcast_to(pn0[:group] - lam * pn0[group:], anew_ref.shape[2:])

    @pl.when(s >= 1)
    def _():
        par = (s - 1) % 2
        for h in range(n_kv):
            @pl.when(c == 0)
            def _():
                a0 = anew_ref[par, h][:, :1].astype(BF16).astype(F32)
                vn = vn_ref[0][:, h * kw:(h + 1) * kw].astype(BF16).astype(F32)
                acc_ref[h] = a0 * vn

            acc_ref[h] += jnp.dot(a_ref[par, h, :, cols], vbuf[slot, h].astype(BF16),
                                  preferred_element_type=F32)

            @pl.when(c == nc - 1)
            def _():
                gain = g_ref[...] * (1.0 - lam_init)
                for g in range(group):
                    lo = (h * group + g) * kw
                    o_ref[0, :, lo:lo + kw] = _rms_norm(acc_ref[h][g:g + 1], gain)


def _diff_decode(pt_flat, q, k_new, v_new, dec_bias, bias0, lam_params, g, cache_k, cache_v, layer,
                 lam_init):
    db, width = q.shape
    depth, n_pool, ps, n_kv, kw = cache_k.shape
    n_pages = pt_flat.shape[0] // db
    pages = min(DEC_PAGES, n_pages)
    assert n_pages % pages == 0
    group = width // (n_kv * kw)
    n_rows = 2 * group
    span = pages * ps
    past = n_pages * ps

    cur = lambda wd: pl.BlockSpec((1, 1, wd), lambda s, c, pt: (jnp.minimum(s, db - 1), 0, 0))
    prev = lambda wd: pl.BlockSpec((1, 1, wd), lambda s, c, pt: (jnp.maximum(s - 1, 0), 0, 0))
    const = lambda shp: pl.BlockSpec(shp, lambda s, c, pt: (0,) * len(shp))
    grid_spec = pltpu.PrefetchScalarGridSpec(
        num_scalar_prefetch=1,
        grid=(db + 1, n_pages // pages),
        in_specs=[cur(width), cur(n_kv * kw), prev(n_kv * kw),
                  pl.BlockSpec((n_kv, n_rows, span), lambda s, c, pt: (0, 0, c)),
                  const((n_kv, n_rows, 1)), const(lam_params.shape), const((1, kw)),
                  pl.BlockSpec(memory_space=pl.ANY), pl.BlockSpec(memory_space=pl.ANY)],
        out_specs=prev(width),
        scratch_shapes=[pltpu.VMEM((2, n_kv, span, kw), F32),
                        pltpu.VMEM((2, n_kv, span, kw), F32),
                        pltpu.SemaphoreType.DMA((2, 2)),
                        pltpu.VMEM((n_kv, n_rows, past), F32),
                        pltpu.VMEM((n_kv, n_rows, LANES), F32),
                        pltpu.VMEM((n_kv, n_rows, LANES), F32),
                        pltpu.VMEM((2, n_kv, group, past), BF16),
                        pltpu.VMEM((2, n_kv, group, LANES), F32),
                        pltpu.VMEM((n_kv, group, kw), F32)],
    )
    out = pl.pallas_call(
        functools.partial(_diff_decode_kernel, scale=(kw // 2) ** -0.5, lam_init=lam_init,
                          n_pages=n_pages, pages=pages, layer=layer, n_seq=db),
        grid_spec=grid_spec,
        out_shape=jax.ShapeDtypeStruct((db, 1, width), F32),
        compiler_params=_params(("arbitrary",) * 2),
    )(pt_flat, q.reshape(db, 1, width), k_new.reshape(db, 1, n_kv * kw), v_new.reshape(db, 1, n_kv * kw),
      dec_bias, bias0, lam_params, g, cache_k, cache_v)
    return out.reshape(db, width)


def _route(logits, bias, n_groups, per_group):
    x = logits + bias
    lane = lax.broadcasted_iota(I32, x.shape, 1).astype(F32)
    big = jnp.float32(4 * LANES)
    neg = jnp.float32(-jnp.inf)

    def first_argmax(vals, mask):
        mx = jnp.max(jnp.where(mask, vals, neg), axis=-1, keepdims=True)
        idx = jnp.min(jnp.where(jnp.logical_and(mask, vals == mx), lane, big), axis=-1, keepdims=True)
        return mx, idx

    gmask = lane < n_groups
    gmax, grp = first_argmax(x, gmask)
    p_sel = 1.0 / jnp.sum(jnp.where(gmask, jnp.exp(x - gmax), 0.0), axis=-1, keepdims=True)
    emask = jnp.logical_and(lane >= n_groups + grp * per_group, lane < n_groups + (grp + 1) * per_group)
    v1, i1 = first_argmax(x, emask)
    v2, i2 = first_argmax(x, jnp.logical_and(emask, lane != i1))
    e2 = jnp.exp(v2 - v1)
    g1 = p_sel / (1.0 + e2)
    g2 = p_sel * e2 / (1.0 + e2)
    out = jnp.where(lane == 0, (i1 - n_groups).astype(F32), 0.0)
    out = jnp.where(lane == 1, (i2 - n_groups).astype(F32), out)
    out = jnp.where(lane == 2, g1, out)
    return jnp.where(lane == 3, g2, out)


def _merge_kernel(osb_ref, od_ref, h_ref, wo_ref, g_ref, b_ref, wr_ref, br_ref, *rest,
                  alpha, n_groups, per_group, aliased, n_tiles):
    h1_ref, r_ref = rest[2:] if aliased else rest
    sbw = osb_ref.shape[1]

    @pl.when(pl.program_id(0) < n_tiles)
    def _():
        att = (jnp.dot(osb_ref[...].astype(BF16), wo_ref[:sbw], preferred_element_type=F32)
               + jnp.dot(od_ref[...].astype(BF16), wo_ref[sbw:], preferred_element_type=F32))
        h1 = _layer_norm(alpha * h_ref[...] + att, g_ref[...], b_ref[...])
        h1_ref[...] = h1
        logits = jnp.dot(h1.astype(BF16), wr_ref[...], preferred_element_type=F32)
        r_ref[...] = _route(logits, br_ref[...], n_groups, per_group)

    @pl.when(pl.program_id(0) >= n_tiles)
    def _():
        h1_ref[...] = jnp.zeros_like(h1_ref)
        r_ref[...] = jnp.zeros_like(r_ref)


def _merge(osb, od, h, wo, g, b, wr, br, total_rows, tile_offset, prev, alpha, n_groups, per_group):
    n, d = h.shape
    tm = ROW_TILE
    assert n % tm == 0 and total_rows % tm == 0
    n_tiles = n // tm
    grid = n_tiles if prev is not None else total_rows // tm
    row = lambda wd: pl.BlockSpec((tm, wd), lambda i: (jnp.minimum(i, n_tiles - 1), 0))
    const = lambda a: pl.BlockSpec(a.shape, lambda i: (0,) * a.ndim, pipeline_mode=pl.Buffered(1))
    out_row = lambda wd: pl.BlockSpec((tm, wd), lambda i: (i + tile_offset, 0))
    args = [osb, od, h, wo, g, b, wr, br]
    in_specs = [row(osb.shape[1]), row(od.shape[1]), row(d), const(wo), const(g), const(b), const(wr),
                const(br)]
    aliases = {}
    if prev is not None:
        args += list(prev)
        in_specs += [pl.BlockSpec(memory_space=pl.ANY)] * 2
        aliases = {8: 0, 9: 1}
    return pl.pallas_call(
        functools.partial(_merge_kernel, alpha=alpha, n_groups=n_groups, per_group=per_group,
                          aliased=prev is not None, n_tiles=n_tiles),
        grid=(grid,),
        in_specs=in_specs,
        out_specs=(out_row(d), out_row(LANES)),
        out_shape=(jax.ShapeDtypeStruct((total_rows, d), F32),
                   jax.ShapeDtypeStruct((total_rows, LANES), F32)),
        input_output_aliases=aliases,
        compiler_params=_params(("arbitrary",)),
    )(*args)


def _moe_kernel(blk_exp_ref, n_used_ref, n_valid_ref, blk_base_ref, order_ref, x_hbm, wgu_ref, wd_ref,
                y_hbm, xbuf, ybuf, gsem, ssem):
    i = pl.program_id(0)
    tm, d = xbuf.shape[1:]
    d_exp = wd_ref.shape[1]
    n_used = n_used_ref[0]
    n_tok = y_hbm.shape[0] - tm
    slot = i % 2

    def gather_copies(blk, s):
        base = blk_base_ref[blk]
        return [pltpu.make_async_copy(x_hbm.at[pl.ds(order_ref[base + r] >> TOP_K_SHIFT, 1)],
                                      xbuf.at[s, pl.ds(r, 1)], gsem.at[s]) for r in range(tm)]

    def scatter_copies(blk, s, n_valid):
        base = blk_base_ref[blk]
        out = []
        for r in range(tm):
            a = order_ref[base + r]
            row = jnp.where(r < n_valid, a >> TOP_K_SHIFT, n_tok + r)
            col = pl.multiple_of(jnp.where(r < n_valid, a & (TOP_K - 1), s) * d, d)
            out.append(pltpu.make_async_copy(ybuf.at[s, pl.ds(r, 1)],
                                             y_hbm.at[pl.ds(row, 1), pl.ds(col, d)], ssem.at[s]))
        return out

    def wait_gather(s):
        pltpu.make_async_copy(x_hbm.at[pl.ds(0, tm)], xbuf.at[s], gsem.at[s]).wait()

    def wait_scatter(s):
        pltpu.make_async_copy(ybuf.at[s], y_hbm.at[pl.ds(0, tm), pl.ds(0, d)], ssem.at[s]).wait()

    @pl.when(i == 0)
    def _():
        xbuf[...] = jnp.zeros_like(xbuf)
        ybuf[...] = jnp.zeros_like(ybuf)
        for k in range(TOP_K):
            fill = pltpu.make_async_copy(ybuf.at[0], y_hbm.at[pl.ds(n_tok, tm), pl.ds(k * d, d)], ssem.at[0])
            fill.start()
            fill.wait()
        for cp in gather_copies(0, 0):
            cp.start()

    @pl.when(jnp.logical_and(i >= 1, i < n_used))
    def _():
        wait_scatter(slot)

    @pl.when(i < n_used)
    def _():
        wait_gather(slot)
        for cp in gather_copies(jnp.minimum(i + 1, n_used - 1), 1 - slot):
            cp.start()
        prev_valid = jnp.where(i >= 1, n_valid_ref[jnp.maximum(i - 1, 0)], 0)
        for cp in scatter_copies(jnp.maximum(i - 1, 0), 1 - slot, prev_valid):
            cp.start()
        x = xbuf[slot].astype(BF16)
        gu = jnp.dot(x, wgu_ref[0], preferred_element_type=F32)
        gate, up = gu[:, :d_exp], gu[:, d_exp:]
        hmid = (gate * jax.nn.sigmoid(gate) * up).astype(BF16)
        ybuf[slot] = jnp.dot(hmid, wd_ref[0], preferred_element_type=F32)

    @pl.when(i == n_used - 1)
    def _():
        wait_gather(1 - slot)
        wait_scatter(1 - slot)
        for cp in scatter_copies(i, slot, n_valid_ref[i]):
            cp.start()
        wait_scatter(slot)


def _moe(blk_exp, n_used, n_valid, blk_base, order, x_all, wgu16, wd16):
    n_blocks = blk_exp.shape[0]
    tm = ROW_TILE
    n_tok, d = x_all.shape
    grid_spec = pltpu.PrefetchScalarGridSpec(
        num_scalar_prefetch=5,
        grid=(n_blocks,),
        in_specs=[pl.BlockSpec(memory_space=pl.ANY),
                  pl.BlockSpec((1,) + wgu16.shape[1:], lambda i, be, *_: (be[i], 0, 0)),
                  pl.BlockSpec((1,) + wd16.shape[1:], lambda i, be, *_: (be[i], 0, 0))],
        out_specs=pl.BlockSpec(memory_space=pl.ANY),
        scratch_shapes=[pltpu.VMEM((2, tm, d), F32), pltpu.VMEM((2, tm, d), F32),
                        pltpu.SemaphoreType.DMA((2,)), pltpu.SemaphoreType.DMA((2,))],
    )
    return pl.pallas_call(
        _moe_kernel,
        grid_spec=grid_spec,
        out_shape=jax.ShapeDtypeStruct((n_tok + tm, TOP_K * d), F32),
        compiler_params=_params(("arbitrary",)),
    )(blk_exp, n_used, n_valid, blk_base, order, x_all, wgu16, wd16)


def _dispatch_tables(expert_idx, n_experts, tm):
    n_tok = expert_idx.shape[0]
    n_assign = n_tok * TOP_K
    flat_e = expert_idx.reshape(-1)
    order = jnp.argsort(flat_e).astype(I32)
    experts = jnp.arange(n_experts, dtype=I32)
    counts = jnp.sum(flat_e[:, None] == experts[None, :], axis=0, dtype=I32)
    start = jnp.cumsum(counts) - counts
    blocks_of = (counts + tm - 1) // tm
    blk_end = jnp.cumsum(blocks_of)
    n_blocks = -(-n_assign // tm) + n_experts
    n_used = blk_end[-1]
    blk = jnp.minimum(jnp.arange(n_blocks, dtype=I32), n_used - 1)
    blk_exp = jnp.sum(blk_end[None, :] <= blk[:, None], axis=1, dtype=I32)
    local = blk - (blk_end - blocks_of)[blk_exp]
    blk_base = start[blk_exp] + local * tm
    n_valid = jnp.clip(counts[blk_exp] - local * tm, 0, tm)
    order = jnp.concatenate([order, jnp.zeros((tm,), I32)])
    return blk_exp, n_used.reshape(1), n_valid, blk_base, order


def _combine_kernel(y_ref, r_ref, h_ref, g_ref, b_ref, o_ref, *, alpha):
    d = h_ref.shape[1]
    r = r_ref[...]
    ffn = y_ref[:, :d] * r[:, 2:3] + y_ref[:, d:] * r[:, 3:4]
    o_ref[...] = _layer_norm(alpha * h_ref[...] + ffn, g_ref[...], b_ref[...])


def _combine(y_pairs, r_all, h_all, g, b, alpha):
    n, d = h_all.shape
    tm = ROW_TILE
    row = lambda wd: pl.BlockSpec((tm, wd), lambda i: (i, 0))
    const = lambda a: pl.BlockSpec(a.shape, lambda i: (0,) * a.ndim)
    return pl.pallas_call(
        functools.partial(_combine_kernel, alpha=alpha),
        grid=(n // tm,),
        in_specs=[row(2 * d), row(LANES), row(d), const(g), const(b)],
        out_specs=row(d),
        out_shape=jax.ShapeDtypeStruct((n, d), F32),
        compiler_params=_params(("arbitrary",)),
    )(y_pairs, r_all, h_all, g, b)


def kernel(x_prompt, x_sample, cache_k_sb, cache_v_sb, cache_k_diff, cache_v_diff, page_table, meta_tokens, ln_in_g, ln_in_b, rel_bias, w_in, sb_norm_g, diff_lambda, diff_subln_g, w_out, ln1_g, ln1_b, w_group, b_group, w_expert_router, b_expert_router, w_gate_up, w_down, ln2_g, ln2_b):
    bsz, seq, d = x_prompt.shape
    db = x_sample.shape[0]
    assert x_sample.shape[1] == 1
    depth = w_in.shape[0]
    assert depth == 1, "the prompt/decode buffers below are laid out for a single layer"
    n_meta = meta_tokens.shape[0]
    hd = sb_norm_g.shape[-1]
    sb_kv, diff_kv = cache_k_sb.shape[3], cache_k_diff.shape[3]
    sb_w, diff_w = d // 2, d // 2
    sb_kv_w, diff_kv_w = sb_kv * hd, diff_kv * 2 * hd
    splits = [0, sb_w, sb_w + sb_kv_w, sb_w + 2 * sb_kv_w, sb_w + 2 * sb_kv_w + diff_w,
              sb_w + 2 * sb_kv_w + diff_w + diff_kv_w, sb_w + 2 * sb_kv_w + diff_w + 2 * diff_kv_w]
    assert splits[-1] == w_in.shape[-1]
    n_groups = w_group.shape[-1]
    n_experts = w_expert_router.shape[-1]
    per_group = n_experts // n_groups
    page = cache_k_sb.shape[2]
    past = page_table.shape[1] * page
    alpha = (2 * depth) ** 0.25
    li = 0
    lam_init = 0.8 - 0.6 * math.exp(-0.3 * li)

    seq_len = n_meta + seq
    lp = -(-seq_len // (LANES * PROJ_TILE // math.gcd(LANES, PROJ_TILE))) * (LANES * PROJ_TILE // math.gcd(LANES, PROJ_TILE))
    xpad = jnp.concatenate([jnp.broadcast_to(meta_tokens[None], (bsz, n_meta, d)), x_prompt,
                            jnp.zeros((bsz, lp - seq_len, d), F32)], axis=1)
    g_in, b_in = ln_in_g.reshape(1, d), ln_in_b.reshape(1, d)
    w_in16 = w_in[li].astype(BF16)
    w_out16 = w_out[li].astype(BF16)
    wgu16 = w_gate_up[li].astype(BF16)
    wd16 = w_down[li].astype(BF16)
    w_router = jnp.zeros((d, LANES), F32).at[:, :n_groups].set(w_group[li]) \
        .at[:, n_groups:n_groups + n_experts].set(w_expert_router[li])
    b_router = jnp.zeros((1, LANES), F32).at[0, :n_groups].set(b_group[li]) \
        .at[0, n_groups:n_groups + n_experts].set(b_expert_router[li])
    w_router = w_router.astype(BF16)
    g1, b1 = ln1_g[li].reshape(1, d), ln1_b[li].reshape(1, d)
    g2, b2 = ln2_g[li].reshape(1, d), ln2_b[li].reshape(1, d)
    g_sb = sb_norm_g[li].reshape(1, hd)
    g_diff = diff_subln_g[li].reshape(1, 2 * hd)
    lam_params = diff_lambda[li]
    idx = jnp.arange(MXU_DIM)
    tri_half = (idx[:LANES, None] >= idx[None, :LANES])
    tri = jnp.concatenate([tri_half, jnp.ones((LANES, LANES), bool)], axis=1).astype(BF16)
    tri = jnp.concatenate([tri, tri], axis=0)

    near, far, dec = _bias_tables(rel_bias, past)

    (hp, qsb16, qd16, ksb16, vsb16, kd16, vd16, k_sb, v_sb, k_d, v_d) = _ln_proj_prompt(
        xpad, g_in, b_in, w_in16, splits, seq_len, sb_kv, diff_kv)
    o_sb = _sb_prompt(qsb16, ksb16, vsb16, tri, g_sb, sb_kv)
    o_d = _diff_prompt(qd16, kd16, vd16, near, far, lam_params, g_diff, diff_kv, lam_init)
    n_prompt = bsz * lp
    total_rows = n_prompt + ROW_TILE
    assert n_prompt % ROW_TILE == 0 and db <= ROW_TILE
    h1_all, r_all = _merge(o_sb.reshape(n_prompt, sb_w), o_d.reshape(n_prompt, diff_w),
                           hp.reshape(n_prompt, d), w_out16, g1, b1, w_router, b_router, total_rows, 0,
                           None, alpha, n_groups, per_group)

    hs, proj_s = _ln_proj_rows(x_sample.reshape(db, d), g_in, b_in, w_in16)
    qn_sb, kn_sb, vn_sb, qn_d, kn_d, vn_d = [proj_s[:, splits[t]:splits[t + 1]] for t in range(6)]
    pt_flat = page_table.reshape(-1).astype(I32)
    os_sb = _sb_decode(pt_flat, qn_sb, tri, g_sb, cache_k_sb, cache_v_sb, li)
    dec_heads = dec[:, 0, :].reshape(diff_kv, -1, past)
    dec_rows = jnp.concatenate([dec_heads] * 2, axis=1)
    bias0 = jnp.concatenate([(rel_bias[0] * LOG2E).reshape(diff_kv, -1, 1)] * 2, axis=1)
    os_d = _diff_decode(pt_flat, qn_d, kn_d, vn_d, dec_rows, bias0, lam_params, g_diff, cache_k_diff,
                        cache_v_diff, li, lam_init)
    pad_rows = lambda a: jnp.concatenate([a, jnp.zeros((ROW_TILE - db, a.shape[1]), a.dtype)], axis=0)
    h1_all, r_all = _merge(pad_rows(os_sb), pad_rows(os_d), pad_rows(hs), w_out16, g1, b1, w_router,
                           b_router, total_rows, n_prompt // ROW_TILE, (h1_all, r_all), alpha, n_groups,
                           per_group)

    expert_idx = r_all[:, :TOP_K].astype(I32)
    tables = _dispatch_tables(expert_idx, n_experts, ROW_TILE)
    y_pairs = _moe(*tables, h1_all, wgu16, wd16)
    y_all = _combine(y_pairs, r_all, h1_all, g2, b2, alpha)

    y_prompt = y_all[:n_prompt].reshape(bsz, lp, d)[:, n_meta:seq_len]
    y_sample = y_all[n_prompt:n_prompt + db].reshape(db, 1, d)
    return (y_prompt, y_sample,
            k_sb[None], v_sb[None], k_d[None], v_d[None],
            kn_sb.reshape(1, db, 1, sb_kv, hd), vn_sb.reshape(1, db, 1, sb_kv, hd),
            kn_d.reshape(1, db, 1, diff_kv, 2 * hd), vn_d.reshape(1, db, 1, diff_kv, 2 * hd))
```

```python
import functools
import math

import jax
import jax.numpy as jnp
from jax import lax
from jax.experimental import pallas as pl
from jax.experimental.pallas import tpu as pltpu

F32 = jnp.float32
BF16 = jnp.bfloat16
I32 = jnp.int32

LANES = 128
MXU_DIM = 256
LN_EPS = 1e-5
REL_BUCKETS, REL_MAX_EXACT, REL_MAX_DIST = 32, 16, 128
TOP_K = 2
TOP_K_SHIFT = TOP_K.bit_length() - 1
LOG2E = math.log2(math.e)
MASKED = -1e30
SB_UNDERFLOW = -104.0
ROW_TILE = 256
PROJ_TILE = 320
DEC_PAGES = 16
FAR_TILES = 8
VMEM_LIMIT = 56 * 1024 * 1024

NT_DIMS = (((1,), (1,)), ((), ()))


def _params(sem, vmem=VMEM_LIMIT):
    return pltpu.CompilerParams(dimension_semantics=sem, vmem_limit_bytes=vmem)


def _layer_norm(x, g, b):
    mu = jnp.mean(x, axis=-1, keepdims=True)
    xc = x - mu
    var = jnp.mean(xc * xc, axis=-1, keepdims=True)
    return xc * lax.rsqrt(var + LN_EPS) * g + b


def _rms_norm(x, g):
    return x * lax.rsqrt(jnp.mean(x * x, axis=-1, keepdims=True) + LN_EPS) * g


def _split_bf16(x):
    hi = x.astype(BF16)
    lo = (x - hi.astype(F32)).astype(BF16)
    return hi, lo


def _lambda_value(lp, lam_init):
    a = jnp.sum(lp[0:1] * lp[1:2], axis=-1, keepdims=True)
    b = jnp.sum(lp[2:3] * lp[3:4], axis=-1, keepdims=True)
    return jnp.exp(a) - jnp.exp(b) + lam_init


def _ln_proj_prompt_kernel(x_ref, g_ref, b_ref, w_ref, h_ref, qsb_ref, qd_ref,
                           ksb16_ref, vsb16_ref, kd16_ref, vd16_ref,
                           ksb_ref, vsb_ref, kd_ref, vd_ref, *, splits):
    h = _layer_norm(x_ref[0], g_ref[...], b_ref[...])
    h_ref[0] = h
    hb = h.astype(BF16)

    def proj(seg):
        lo, hi = splits[seg], splits[seg + 1]
        return jnp.dot(hb, w_ref[:, lo:hi], preferred_element_type=F32)

    def emit(seg, full_ref, half_ref):
        y = proj(seg)
        width = full_ref.shape[3]
        for h in range(full_ref.shape[2]):
            full_ref[0, :, h, :] = y[:, h * width:(h + 1) * width]
        half_ref[0] = y.astype(BF16)

    qsb_ref[0] = proj(0).astype(BF16)
    emit(1, ksb_ref, ksb16_ref)
    emit(2, vsb_ref, vsb16_ref)
    qd_ref[0] = proj(3).astype(BF16)
    emit(4, kd_ref, kd16_ref)
    emit(5, vd_ref, vd16_ref)


def _ln_proj_prompt(xpad, g, b, w16, splits, seq_len, sb_kv, diff_kv):
    bsz, lp, d = xpad.shape
    widths = [splits[i + 1] - splits[i] for i in range(6)]
    tm = PROJ_TILE
    assert lp % tm == 0 and lp - seq_len < tm
    row = lambda wd: pl.BlockSpec((1, tm, wd), lambda bi, i: (bi, i, 0))
    heads = lambda n, wd: pl.BlockSpec((1, tm, n, wd // n), lambda bi, i: (bi, i, 0, 0))
    const = lambda shp: pl.BlockSpec(shp, lambda bi, i: (0,) * len(shp))
    out_shape = (
        jax.ShapeDtypeStruct((bsz, lp, d), F32),
        jax.ShapeDtypeStruct((bsz, lp, widths[0]), BF16),
        jax.ShapeDtypeStruct((bsz, lp, widths[3]), BF16),
        jax.ShapeDtypeStruct((bsz, lp, widths[1]), BF16),
        jax.ShapeDtypeStruct((bsz, lp, widths[2]), BF16),
        jax.ShapeDtypeStruct((bsz, lp, widths[4]), BF16),
        jax.ShapeDtypeStruct((bsz, lp, widths[5]), BF16),
        jax.ShapeDtypeStruct((bsz, seq_len, sb_kv, widths[1] // sb_kv), F32),
        jax.ShapeDtypeStruct((bsz, seq_len, sb_kv, widths[2] // sb_kv), F32),
        jax.ShapeDtypeStruct((bsz, seq_len, diff_kv, widths[4] // diff_kv), F32),
        jax.ShapeDtypeStruct((bsz, seq_len, diff_kv, widths[5] // diff_kv), F32),
    )
    out_specs = (row(d), row(widths[0]), row(widths[3]), row(widths[1]), row(widths[2]),
                 row(widths[4]), row(widths[5]), heads(sb_kv, widths[1]), heads(sb_kv, widths[2]),
                 heads(diff_kv, widths[4]), heads(diff_kv, widths[5]))
    return pl.pallas_call(
        functools.partial(_ln_proj_prompt_kernel, splits=tuple(splits)),
        grid=(bsz, lp // tm),
        in_specs=[row(d), const((1, d)), const((1, d)),
                  pl.BlockSpec(w16.shape, lambda bi, i: (0, 0), pipeline_mode=pl.Buffered(1))],
        out_specs=out_specs,
        out_shape=out_shape,
        compiler_params=_params(("arbitrary", "arbitrary")),
    )(xpad, g, b, w16)


def _ln_proj_rows_kernel(x_ref, g_ref, b_ref, w_ref, h_ref, p_ref):
    h = _layer_norm(x_ref[...], g_ref[...], b_ref[...])
    h_ref[...] = h
    p_ref[...] = jnp.dot(h.astype(BF16), w_ref[...], preferred_element_type=F32)


def _ln_proj_rows(x, g, b, w):
    n, d = x.shape
    pw = w.shape[1]
    tn = 512
    assert pw % tn == 0
    return pl.pallas_call(
        _ln_proj_rows_kernel,
        grid=(pw // tn,),
        in_specs=[pl.BlockSpec((n, d), lambda j: (0, 0)), pl.BlockSpec((1, d), lambda j: (0, 0)),
                  pl.BlockSpec((1, d), lambda j: (0, 0)), pl.BlockSpec((d, tn), lambda j: (0, j))],
        out_specs=(pl.BlockSpec((n, d), lambda j: (0, 0)), pl.BlockSpec((n, tn), lambda j: (0, j))),
        out_shape=(jax.ShapeDtypeStruct((n, d), F32), jax.ShapeDtypeStruct((n, pw), F32)),
        compiler_params=_params(("arbitrary",)),
    )(x, g, b, w)


def _bias_from_dist(dist, rel_ref, head):
    n = jnp.maximum(dist, 0)
    nf = jnp.maximum(n, 1).astype(F32)
    large = REL_MAX_EXACT + (jnp.log(nf / REL_MAX_EXACT) / math.log(REL_MAX_DIST / REL_MAX_EXACT)
                             * (REL_BUCKETS - REL_MAX_EXACT)).astype(I32)
    large = jnp.minimum(large, REL_BUCKETS - 1)
    bucket = jnp.where(n < REL_MAX_EXACT, n, large)
    out = jnp.zeros(dist.shape, F32)
    for bkt in range(REL_BUCKETS):
        out = jnp.where(bucket == bkt, rel_ref[bkt, head], out)
    return out * LOG2E


def _bias_tables_kernel(rel_ref, near_ref, far_ref, dec_ref, *, n_heads, past):
    i = lax.broadcasted_iota(I32, (LANES, 2 * LANES), 0)
    j = lax.broadcasted_iota(I32, (LANES, 2 * LANES), 1)
    dist = i + LANES - j
    kpos = lax.broadcasted_iota(I32, dec_ref.shape[1:], 1)
    for h in range(n_heads):
        near_ref[h] = jnp.where(dist >= 0, _bias_from_dist(dist, rel_ref, h), MASKED)
        far_ref[h] = jnp.full(far_ref.shape[1:], rel_ref[REL_BUCKETS - 1, h] * LOG2E, F32)
        dec_ref[h] = _bias_from_dist(past - kpos, rel_ref, h)


def _bias_tables(rel_bias, past):
    n_heads = rel_bias.shape[1]
    return pl.pallas_call(
        functools.partial(_bias_tables_kernel, n_heads=n_heads, past=past),
        in_specs=[pl.BlockSpec(memory_space=pltpu.SMEM)],
        out_shape=(jax.ShapeDtypeStruct((n_heads, LANES, 2 * LANES), F32),
                   jax.ShapeDtypeStruct((n_heads, LANES, LANES), F32),
                   jax.ShapeDtypeStruct((n_heads, 8, past), F32)),
    )(rel_bias)


def _sb_tile(z, vis, tri, carry):
    l = -(jnp.maximum(z, 0.0) + jnp.log(1.0 + jnp.exp(-jnp.abs(z))))
    if vis is not None:
        l = jnp.where(vis, l, 0.0)
    hi, lo = _split_bf16(l)
    t2 = jnp.dot(jnp.concatenate([hi, lo], axis=1), tri, preferred_element_type=F32)
    nt = z.shape[1]
    incl, total = t2[:, :nt], t2[:, nt:]
    w = jnp.exp(z + incl + carry)
    if vis is not None:
        w = jnp.where(vis, w, 0.0)
    return w, carry + total


def _sb_prompt_kernel(q_ref, k_ref, v_ref, tri_ref, g_ref, o_ref, acc_ref, carry_ref, *, scale):
    qi = pl.program_id(0)
    bsz, n_kv, m, hd = acc_ref.shape
    group = m // LANES
    qh = [[jnp.concatenate([q_ref[b, :, (h * group + g) * hd:(h * group + g + 1) * hd]
                            for g in range(group)], axis=0) for h in range(n_kv)] for b in range(bsz)]
    acc_ref[...] = jnp.zeros_like(acc_ref)
    carry_ref[...] = jnp.zeros_like(carry_ref)
    rowpos = lax.broadcasted_iota(I32, (m, LANES), 0) & (LANES - 1)
    col = lax.broadcasted_iota(I32, (m, LANES), 1)
    tri = tri_ref[...]

    def cond(state):
        j, live = state
        return jnp.logical_and(j >= 0, live > SB_UNDERFLOW)

    def body(state):
        j, _ = state
        ks = pl.multiple_of(j * LANES, LANES)
        vis = col < rowpos + (qi - j) * LANES
        live = None
        for b in range(bsz):
            for h in range(n_kv):
                k = k_ref[b, pl.ds(ks, LANES), h * hd:(h + 1) * hd]
                v = v_ref[b, pl.ds(ks, LANES), h * hd:(h + 1) * hd]
                z = lax.dot_general(qh[b][h], k, NT_DIMS, preferred_element_type=F32) * scale
                w, carry = _sb_tile(z, vis, tri, carry_ref[b, h])
                acc_ref[b, h] += jnp.dot(w.astype(BF16), v, preferred_element_type=F32)
                carry_ref[b, h] = carry
                live = carry if live is None else jnp.maximum(live, carry)
        return j - 1, jnp.max(live)

    lax.while_loop(cond, body, (qi, jnp.float32(0.0)))
    for b in range(bsz):
        outs = []
        for h in range(n_kv):
            o = _rms_norm(acc_ref[b, h], g_ref[...]).astype(o_ref.dtype)
            outs += [o[g * LANES:(g + 1) * LANES] for g in range(group)]
        o_ref[b] = jnp.concatenate(outs, axis=1)


def _sb_prompt(q16, k16, v16, tri, g, n_kv):
    bsz, lp, width = q16.shape
    kvw = k16.shape[2]
    hd = kvw // n_kv
    m = width // n_kv // hd * LANES
    resident = lambda: pl.BlockSpec((bsz, lp, kvw), lambda i: (0, 0, 0), pipeline_mode=pl.Buffered(1))
    return pl.pallas_call(
        functools.partial(_sb_prompt_kernel, scale=hd ** -0.5),
        grid=(lp // LANES,),
        in_specs=[pl.BlockSpec((bsz, LANES, width), lambda i: (0, i, 0)),
                  resident(), resident(),
                  pl.BlockSpec(tri.shape, lambda i: (0, 0)),
                  pl.BlockSpec((1, hd), lambda i: (0, 0))],
        out_specs=pl.BlockSpec((bsz, LANES, width), lambda i: (0, i, 0)),
        out_shape=jax.ShapeDtypeStruct((bsz, lp, width), BF16),
        scratch_shapes=[pltpu.VMEM((bsz, n_kv, m, hd), F32), pltpu.VMEM((bsz, n_kv, m, LANES), F32)],
        compiler_params=_params(("arbitrary",)),
    )(q16, k16, v16, tri, g)


def _sb_decode_kernel(pt_ref, q_ref, tri_ref, g_ref, k_hbm, v_hbm, o_ref, kbuf, vbuf, sem, *,
                      scale, n_kv, n_pages, layer):
    s = pl.program_id(0)
    q = q_ref[0]
    hd = g_ref.shape[1]
    n_heads = q.shape[1] // hd
    group = n_heads // n_kv
    zeros = jnp.zeros((1, hd), F32)
    rows = []
    for n in range(n_heads):
        seg = q[:, n * hd:(n + 1) * hd]
        rows.append(jnp.concatenate([seg if h == n // group else zeros for h in range(n_kv)], axis=1))
    qb = jnp.concatenate(rows, axis=0)
    q16 = qb.astype(BF16)
    tri = tri_ref[...]

    def page_copies(j, slot):
        page = pt_ref[s * n_pages + j]
        return [pltpu.make_async_copy(hbm.at[layer, page, :, h, :], buf.at[slot, h], sem.at[which, slot])
                for which, (hbm, buf) in enumerate(((k_hbm, kbuf), (v_hbm, vbuf))) for h in range(n_kv)]

    def fetch(j, slot):
        for cp in page_copies(j, slot):
            cp.start()

    def wait_page(j, slot):
        for cp in page_copies(j, slot):
            cp.wait()

    def heads_on_lanes(buf, slot):
        return jnp.concatenate([buf[slot, h] for h in range(n_kv)], axis=1)

    fetch(n_pages - 1, 0)

    def cond(state):
        j, live, _, _ = state
        return jnp.logical_and(j >= 0, live > SB_UNDERFLOW)

    def body(state):
        j, _, carry, acc = state
        slot = (n_pages - 1 - j) % 2
        wait_page(j, slot)

        @pl.when(j > 0)
        def _():
            fetch(j - 1, 1 - slot)

        k = heads_on_lanes(kbuf, slot).astype(BF16)
        v = heads_on_lanes(vbuf, slot).astype(BF16)
        z = lax.dot_general(q16, k, NT_DIMS, preferred_element_type=F32) * scale
        w, carry = _sb_tile(z, None, tri, carry)
        acc = acc + jnp.dot(w.astype(BF16), v, preferred_element_type=F32)
        return j - 1, jnp.max(carry), carry, acc

    init = (jnp.int32(n_pages - 1), jnp.float32(0.0), jnp.zeros((n_heads, LANES), F32),
            jnp.zeros((n_heads, n_kv * hd), F32))
    j_end, _, _, acc = lax.while_loop(cond, body, init)

    @pl.when(j_end >= 0)
    def _():
        wait_page(j_end, (n_pages - 1 - j_end) % 2)

    outs = []
    for n in range(n_heads):
        h = n // group
        outs.append(_rms_norm(acc[n:n + 1, h * hd:(h + 1) * hd], g_ref[...]))
    o_ref[0] = jnp.concatenate(outs, axis=1)


def _sb_decode(pt_flat, q, tri, g, cache_k, cache_v, layer):
    db, width = q.shape
    depth, n_pool, ps, n_kv, hd = cache_k.shape
    assert ps == LANES
    n_pages = pt_flat.shape[0] // db
    grid_spec = pltpu.PrefetchScalarGridSpec(
        num_scalar_prefetch=1,
        grid=(db,),
        in_specs=[pl.BlockSpec((1, 1, width), lambda s, pt: (s, 0, 0)),
                  pl.BlockSpec(tri.shape, lambda s, pt: (0, 0)),
                  pl.BlockSpec((1, hd), lambda s, pt: (0, 0)),
                  pl.BlockSpec(memory_space=pl.ANY), pl.BlockSpec(memory_space=pl.ANY)],
        out_specs=pl.BlockSpec((1, 1, width), lambda s, pt: (s, 0, 0)),
        scratch_shapes=[pltpu.VMEM((2, n_kv, ps, hd), F32), pltpu.VMEM((2, n_kv, ps, hd), F32),
                        pltpu.SemaphoreType.DMA((2, 2))],
    )
    out = pl.pallas_call(
        functools.partial(_sb_decode_kernel, scale=hd ** -0.5, n_kv=n_kv, n_pages=n_pages, layer=layer),
        grid_spec=grid_spec,
        out_shape=jax.ShapeDtypeStruct((db, 1, width), F32),
        compiler_params=_params(("arbitrary",)),
    )(pt_flat, q.reshape(db, 1, width), tri, g, cache_k, cache_v)
    return out.reshape(db, width)


def _diff_prompt_kernel(q_ref, k_ref, v_ref, near_ref, far_ref, lam_ref, g_ref, o_ref,
                        m_ref, l_ref, acc_ref, *, scale, lam_init, far_tiles):
    qi = pl.program_id(1)
    n_kv = m_ref.shape[0]
    kw = k_ref.shape[2] // n_kv
    hd = kw // 2
    group = q_ref.shape[2] // (n_kv * kw)
    rows = group * LANES
    qm = [[jnp.concatenate([q_ref[0, :, ((h * group + g) * 2 + c) * hd:((h * group + g) * 2 + c + 1) * hd]
                            for g in range(group)], axis=0) for c in range(2)] for h in range(n_kv)]
    c1 = scale * LOG2E

    def scores(h, c, ks, width):
        k = k_ref[0, pl.ds(ks, width), h * kw + c * hd:h * kw + (c + 1) * hd]
        return lax.dot_general(qm[h][c], k, NT_DIMS, preferred_element_type=F32) * c1

    def lane_tiles(x):
        return [x[:, t * LANES:(t + 1) * LANES] for t in range(x.shape[1] // LANES)]

    def row_max(s):
        return jnp.max(functools.reduce(jnp.maximum, lane_tiles(s)), axis=-1, keepdims=True)

    def exp_tiles(s, shift):
        ps = [jnp.exp2(t - shift) for t in lane_tiles(s)]
        return jnp.concatenate(ps, axis=1).astype(BF16), functools.reduce(jnp.add, ps)

    def near_chunk(ks, first_col, width):
        for h in range(n_kv):
            v = v_ref[0, pl.ds(ks, width), h * kw:(h + 1) * kw]
            bias = near_ref[h, :, first_col:first_col + width]
            for c in range(2):
                s = scores(h, c, ks, width) + bias
                mx = jnp.broadcast_to(row_max(s), (rows, LANES))
                p, psum = exp_tiles(s, mx)
                m_ref[h, 0, c] = mx
                l_ref[h, 0, c] = psum
                acc_ref[h, 0, c] = jnp.dot(p, v, preferred_element_type=F32)

    @pl.when(qi == 0)
    def _():
        near_chunk(0, LANES, LANES)

    @pl.when(qi > 0)
    def _():
        near_chunk(pl.multiple_of((qi - 1) * LANES, LANES), 0, 2 * LANES)

    for h in range(n_kv):
        m_ref[h, 1] = jnp.full(m_ref.shape[2:], MASKED, F32)
        l_ref[h, 1] = jnp.zeros(l_ref.shape[2:], F32)
        acc_ref[h, 1] = jnp.zeros(acc_ref.shape[2:], F32)

    def far_chunk(ks, stream, key_range=None):
        for h in range(n_kv):
            v = v_ref[0, pl.ds(ks, wide), h * kw:(h + 1) * kw]
            fb = far_ref[h]
            for c in range(2):
                s = scores(h, c, ks, wide)
                if key_range is not None:
                    kpos = ks + lax.broadcasted_iota(I32, s.shape, 1)
                    s = jnp.where(jnp.logical_and(kpos >= key_range[0], kpos < key_range[1]), s, MASKED)
                m_old = m_ref[h, stream, c]
                m_new = jnp.maximum(m_old, row_max(s) + fb)
                p, psum = exp_tiles(s, m_new - fb)
                pv = jnp.dot(p, v, preferred_element_type=F32)
                alpha = jnp.exp2(m_old - m_new)
                l_ref[h, stream, c] = alpha * l_ref[h, stream, c] + psum
                acc_ref[h, stream, c] = (jnp.concatenate([alpha] * (kw // LANES), axis=1) * acc_ref[h, stream, c]
                                         + pv)
                m_ref[h, stream, c] = m_new

    n_far = jnp.maximum(qi - 1, 0)
    wide = far_tiles * LANES
    n_full = n_far // far_tiles

    def pair_body(t, carry):
        far_chunk(pl.multiple_of(2 * t * wide, wide), 0)
        far_chunk(pl.multiple_of((2 * t + 1) * wide, wide), 1)
        return carry

    lax.fori_loop(0, n_full // 2, pair_body, 0)

    @pl.when(n_full % 2 == 1)
    def _():
        far_chunk(pl.multiple_of((n_full - 1) * wide, wide), 0)

    @pl.when(n_far % far_tiles != 0)
    def _():
        ks = pl.multiple_of(jnp.minimum(n_full * wide, k_ref.shape[1] - wide), LANES)
        far_chunk(ks, 1, (n_full * wide, n_far * LANES))

    lam = _lambda_value(lam_ref[...], lam_init)
    outs = []
    for h in range(n_kv):
        merged = []
        for c in range(2):
            m = jnp.maximum(m_ref[h, 0, c], m_ref[h, 1, c])
            a0, a1 = jnp.exp2(m_ref[h, 0, c] - m), jnp.exp2(m_ref[h, 1, c] - m)
            l = jnp.sum(a0 * l_ref[h, 0, c] + a1 * l_ref[h, 1, c], axis=-1, keepdims=True)
            merged.append((a0[:, :1] * acc_ref[h, 0, c] + a1[:, :1] * acc_ref[h, 1, c]) / l)
        a = merged[0] - lam * merged[1]
        o = _rms_norm(a, g_ref[...] * (1.0 - lam_init)).astype(o_ref.dtype)
        outs += [o[g * LANES:(g + 1) * LANES] for g in range(group)]
    o_ref[0] = jnp.concatenate(outs, axis=1)


def _diff_prompt(q16, k16, v16, near, far, lam_params, g, n_kv, lam_init):
    bsz, lp, width = q16.shape
    kvw = k16.shape[2]
    kw = kvw // n_kv
    group = width // kvw
    rows = group * LANES
    nq = lp // LANES
    far_tiles = min(FAR_TILES, nq)
    near_g = near.reshape(n_kv, rows, 2 * LANES)
    far_g = far.reshape(n_kv, rows, LANES)
    resident = lambda: pl.BlockSpec((1, lp, kvw), lambda b, i: (b, 0, 0), pipeline_mode=pl.Buffered(1))
    const = lambda a: pl.BlockSpec(a.shape, lambda b, i: (0,) * a.ndim)
    return pl.pallas_call(
        functools.partial(_diff_prompt_kernel, scale=(kw // 2) ** -0.5, lam_init=lam_init,
                          far_tiles=far_tiles),
        grid=(bsz, nq),
        in_specs=[pl.BlockSpec((1, LANES, width), lambda b, i: (b, i, 0)), resident(), resident(),
                  const(near_g), const(far_g), const(lam_params), const(g)],
        out_specs=pl.BlockSpec((1, LANES, width), lambda b, i: (b, i, 0)),
        out_shape=jax.ShapeDtypeStruct((bsz, lp, width), BF16),
        scratch_shapes=[pltpu.VMEM((n_kv, 2, 2, rows, LANES), F32), pltpu.VMEM((n_kv, 2, 2, rows, LANES), F32),
                        pltpu.VMEM((n_kv, 2, 2, rows, kw), F32)],
        compiler_params=_params(("arbitrary",) * 2),
    )(q16, k16, v16, near_g, far_g, lam_params, g)


def _diff_decode_kernel(pt_ref, q_ref, kn_ref, vn_ref, bias_ref, bias0_ref, lam_ref, g_ref, k_hbm, v_hbm,
                        o_ref, kbuf, vbuf, sem, s_ref, snew_ref, m_ref, a_ref, anew_ref, acc_ref, *,
                        scale, lam_init, n_pages, pages, layer, n_seq):
    s = pl.program_id(0)
    c = pl.program_id(1)
    nc = n_pages // pages
    n_kv, span, kw = kbuf.shape[1:]
    ps = span // pages
    hd = kw // 2
    group = q_ref.shape[2] // (n_kv * kw)
    step = s * nc + c
    slot = step % 2
    c1 = scale * LOG2E

    def chunk_copies(step_idx, dst_slot):
        seq, ch = step_idx // nc, step_idx % nc
        base_k = jnp.minimum(seq, n_seq - 1) * n_pages + ch * pages
        base_v = jnp.maximum(seq - 1, 0) * n_pages + ch * pages
        out = []
        for t in range(pages):
            for which, (hbm, buf, base) in enumerate(((k_hbm, kbuf, base_k), (v_hbm, vbuf, base_v))):
                page = pt_ref[base + t]
                out += [pltpu.make_async_copy(hbm.at[layer, page, :, h, :],
                                              buf.at[dst_slot, h, pl.ds(t * ps, ps)], sem.at[which, dst_slot])
                        for h in range(n_kv)]
        return out

    @pl.when(step == 0)
    def _():
        for cp in chunk_copies(step, slot):
            cp.start()

    @pl.when(step + 1 < (n_seq + 1) * nc)
    def _():
        for cp in chunk_copies(step + 1, 1 - slot):
            cp.start()

    for cp in chunk_copies(step, slot):
        cp.wait()

    lam = _lambda_value(lam_ref[...], lam_init)
    cols = pl.ds(pl.multiple_of(c * span, span), span)

    @pl.when(s < n_seq)
    def _():
        q = q_ref[0]
        zeros = jnp.zeros((1, hd), F32)
        for h in range(n_kv):
            rows = []
            for mp in range(2):
                for g in range(group):
                    lo = ((h * group + g) * 2 + mp) * hd
                    rows.append(jnp.concatenate([q[:, lo:lo + hd], zeros] if mp == 0 else
                                                [zeros, q[:, lo:lo + hd]], axis=1))
            qb = jnp.concatenate(rows, axis=0).astype(BF16)

            @pl.when(c == 0)
            def _():
                kn = kn_ref[0][:, h * kw:(h + 1) * kw].astype(BF16)
                s0 = jnp.sum(qb.astype(F32) * kn.astype(F32), axis=-1, keepdims=True) * c1 + bias0_ref[h]
                snew_ref[h] = jnp.broadcast_to(s0, snew_ref.shape[1:])
                m_ref[h] = jnp.broadcast_to(s0, m_ref.shape[1:])

            sc = lax.dot_general(qb, kbuf[slot, h].astype(BF16), NT_DIMS,
                                 preferred_element_type=F32) * c1 + bias_ref[h]
            s_ref[h, :, cols] = sc
            m_ref[h] = jnp.maximum(m_ref[h], jnp.max(sc, axis=-1, keepdims=True))

            @pl.when(c == nc - 1)
            def _():
                m = m_ref[h][:, :1]
                p = jnp.exp2(s_ref[h] - m)
                p0 = jnp.exp2(snew_ref[h][:, :1] - m)
                l = jnp.sum(p, axis=-1, keepdims=True) + p0
                pn, pn0 = p / l, p0 / l
                a_ref[s % 2, h] = (pn[:group] - lam * pn[group:]).astype(a_ref.dtype)
                anew_ref[s % 2, h] = jnp.broadcast_to(pn0[:group] - lam * pn0[group:], anew_ref.shape[2:])

    @pl.when(s >= 1)
    def _():
        par = (s - 1) % 2
        for h in range(n_kv):
            @pl.when(c == 0)
            def _():
                a0 = anew_ref[par, h][:, :1].astype(BF16).astype(F32)
                vn = vn_ref[0][:, h * kw:(h + 1) * kw].astype(BF16).astype(F32)
                acc_ref[h] = a0 * vn

            acc_ref[h] += jnp.dot(a_ref[par, h, :, cols], vbuf[slot, h].astype(BF16),
                                  preferred_element_type=F32)

            @pl.when(c == nc - 1)
            def _():
                gain = g_ref[...] * (1.0 - lam_init)
                for g in range(group):
                    lo = (h * group + g) * kw
                    o_ref[0, :, lo:lo + kw] = _rms_norm(acc_ref[h][g:g + 1], gain)


def _diff_decode(pt_flat, q, k_new, v_new, dec_bias, bias0, lam_params, g, cache_k, cache_v, layer,
                 lam_init):
    db, width = q.shape
    depth, n_pool, ps, n_kv, kw = cache_k.shape
    n_pages = pt_flat.shape[0] // db
    pages = min(DEC_PAGES, n_pages)
    assert n_pages % pages == 0
    group = width // (n_kv * kw)
    n_rows = 2 * group
    span = pages * ps
    past = n_pages * ps

    cur = lambda wd: pl.BlockSpec((1, 1, wd), lambda s, c, pt: (jnp.minimum(s, db - 1), 0, 0))
    prev = lambda wd: pl.BlockSpec((1, 1, wd), lambda s, c, pt: (jnp.maximum(s - 1, 0), 0, 0))
    const = lambda shp: pl.BlockSpec(shp, lambda s, c, pt: (0,) * len(shp))
    grid_spec = pltpu.PrefetchScalarGridSpec(
        num_scalar_prefetch=1,
        grid=(db + 1, n_pages // pages),
        in_specs=[cur(width), cur(n_kv * kw), prev(n_kv * kw),
                  pl.BlockSpec((n_kv, n_rows, span), lambda s, c, pt: (0, 0, c)),
                  const((n_kv, n_rows, 1)), const(lam_params.shape), const((1, kw)),
                  pl.BlockSpec(memory_space=pl.ANY), pl.BlockSpec(memory_space=pl.ANY)],
        out_specs=prev(width),
        scratch_shapes=[pltpu.VMEM((2, n_kv, span, kw), F32),
                        pltpu.VMEM((2, n_kv, span, kw), F32),
                        pltpu.SemaphoreType.DMA((2, 2)),
                        pltpu.VMEM((n_kv, n_rows, past), F32),
                        pltpu.VMEM((n_kv, n_rows, LANES), F32),
                        pltpu.VMEM((n_kv, n_rows, LANES), F32),
                        pltpu.VMEM((2, n_kv, group, past), BF16),
                        pltpu.VMEM((2, n_kv, group, LANES), F32),
                        pltpu.VMEM((n_kv, group, kw), F32)],
    )
    out = pl.pallas_call(
        functools.partial(_diff_decode_kernel, scale=(kw // 2) ** -0.5, lam_init=lam_init,
                          n_pages=n_pages, pages=pages, layer=layer, n_seq=db),
        grid_spec=grid_spec,
        out_shape=jax.ShapeDtypeStruct((db, 1, width), F32),
        compiler_params=_params(("arbitrary",) * 2),
    )(pt_flat, q.reshape(db, 1, width), k_new.reshape(db, 1, n_kv * kw), v_new.reshape(db, 1, n_kv * kw),
      dec_bias, bias0, lam_params, g, cache_k, cache_v)
    return out.reshape(db, width)


def _route(logits, bias, n_groups, per_group):
    x = logits + bias
    lane = lax.broadcasted_iota(I32, x.shape, 1).astype(F32)
    big = jnp.float32(4 * LANES)
    neg = jnp.float32(-jnp.inf)

    def first_argmax(vals, mask):
        mx = jnp.max(jnp.where(mask, vals, neg), axis=-1, keepdims=True)
        idx = jnp.min(jnp.where(jnp.logical_and(mask, vals == mx), lane, big), axis=-1, keepdims=True)
        return mx, idx

    gmask = lane < n_groups
    gmax, grp = first_argmax(x, gmask)
    p_sel = 1.0 / jnp.sum(jnp.where(gmask, jnp.exp(x - gmax), 0.0), axis=-1, keepdims=True)
    emask = jnp.logical_and(lane >= n_groups + grp * per_group, lane < n_groups + (grp + 1) * per_group)
    v1, i1 = first_argmax(x, emask)
    v2, i2 = first_argmax(x, jnp.logical_and(emask, lane != i1))
    e2 = jnp.exp(v2 - v1)
    g1 = p_sel / (1.0 + e2)
    g2 = p_sel * e2 / (1.0 + e2)
    out = jnp.where(lane == 0, (i1 - n_groups).astype(F32), 0.0)
    out = jnp.where(lane == 1, (i2 - n_groups).astype(F32), out)
    out = jnp.where(lane == 2, g1, out)
    return jnp.where(lane == 3, g2, out)


def _merge_kernel(osb_ref, od_ref, h_ref, wo_ref, g_ref, b_ref, wr_ref, br_ref, *rest,
                  alpha, n_groups, per_group, aliased, n_tiles):
    h1_ref, r_ref = rest[2:] if aliased else rest
    sbw = osb_ref.shape[1]

    @pl.when(pl.program_id(0) < n_tiles)
    def _():
        att = (jnp.dot(osb_ref[...].astype(BF16), wo_ref[:sbw], preferred_element_type=F32)
               + jnp.dot(od_ref[...].astype(BF16), wo_ref[sbw:], preferred_element_type=F32))
        h1 = _layer_norm(alpha * h_ref[...] + att, g_ref[...], b_ref[...])
        h1_ref[...] = h1
        logits = jnp.dot(h1.astype(BF16), wr_ref[...], preferred_element_type=F32)
        r_ref[...] = _route(logits, br_ref[...], n_groups, per_group)

    @pl.when(pl.program_id(0) >= n_tiles)
    def _():
        h1_ref[...] = jnp.zeros_like(h1_ref)
        r_ref[...] = jnp.zeros_like(r_ref)


def _merge(osb, od, h, wo, g, b, wr, br, total_rows, tile_offset, prev, alpha, n_groups, per_group):
    n, d = h.shape
    tm = ROW_TILE
    assert n % tm == 0 and total_rows % tm == 0
    n_tiles = n // tm
    grid = n_tiles if prev is not None else total_rows // tm
    row = lambda wd: pl.BlockSpec((tm, wd), lambda i: (jnp.minimum(i, n_tiles - 1), 0))
    const = lambda a: pl.BlockSpec(a.shape, lambda i: (0,) * a.ndim, pipeline_mode=pl.Buffered(1))
    out_row = lambda wd: pl.BlockSpec((tm, wd), lambda i: (i + tile_offset, 0))
    args = [osb, od, h, wo, g, b, wr, br]
    in_specs = [row(osb.shape[1]), row(od.shape[1]), row(d), const(wo), const(g), const(b), const(wr),
                const(br)]
    aliases = {}
    if prev is not None:
        args += list(prev)
        in_specs += [pl.BlockSpec(memory_space=pl.ANY)] * 2
        aliases = {8: 0, 9: 1}
    return pl.pallas_call(
        functools.partial(_merge_kernel, alpha=alpha, n_groups=n_groups, per_group=per_group,
                          aliased=prev is not None, n_tiles=n_tiles),
        grid=(grid,),
        in_specs=in_specs,
        out_specs=(out_row(d), out_row(LANES)),
        out_shape=(jax.ShapeDtypeStruct((total_rows, d), F32),
                   jax.ShapeDtypeStruct((total_rows, LANES), F32)),
        input_output_aliases=aliases,
        compiler_params=_params(("arbitrary",)),
    )(*args)


def _moe_kernel(blk_exp_ref, n_used_ref, n_valid_ref, blk_base_ref, order_ref, x_hbm, wgu_ref, wd_ref,
                y_hbm, xbuf, ybuf, gsem, ssem):
    i = pl.program_id(0)
    tm, d = xbuf.shape[1:]
    d_exp = wd_ref.shape[1]
    n_used = n_used_ref[0]
    n_tok = y_hbm.shape[0] - tm
    slot = i % 2

    def gather_copies(blk, s):
        base = blk_base_ref[blk]
        return [pltpu.make_async_copy(x_hbm.at[pl.ds(order_ref[base + r] >> TOP_K_SHIFT, 1)],
                                      xbuf.at[s, pl.ds(r, 1)], gsem.at[s]) for r in range(tm)]

    def scatter_copies(blk, s, n_valid):
        base = blk_base_ref[blk]
        out = []
        for r in range(tm):
            a = order_ref[base + r]
            row = jnp.where(r < n_valid, a >> TOP_K_SHIFT, n_tok + r)
            col = pl.multiple_of(jnp.where(r < n_valid, a & (TOP_K - 1), s) * d, d)
            out.append(pltpu.make_async_copy(ybuf.at[s, pl.ds(r, 1)],
                                             y_hbm.at[pl.ds(row, 1), pl.ds(col, d)], ssem.at[s]))
        return out

    def wait_gather(s):
        pltpu.make_async_copy(x_hbm.at[pl.ds(0, tm)], xbuf.at[s], gsem.at[s]).wait()

    def wait_scatter(s):
        pltpu.make_async_copy(ybuf.at[s], y_hbm.at[pl.ds(0, tm), pl.ds(0, d)], ssem.at[s]).wait()

    @pl.when(i == 0)
    def _():
        xbuf[...] = jnp.zeros_like(xbuf)
        ybuf[...] = jnp.zeros_like(ybuf)
        for k in range(TOP_K):
            fill = pltpu.make_async_copy(ybuf.at[0], y_hbm.at[pl.ds(n_tok, tm), pl.ds(k * d, d)], ssem.at[0])
            fill.start()
            fill.wait()
        for cp in gather_copies(0, 0):
            cp.start()

    @pl.when(jnp.logical_and(i >= 1, i < n_used))
    def _():
        wait_scatter(slot)

    @pl.when(i < n_used)
    def _():
        wait_gather(slot)
        for cp in gather_copies(jnp.minimum(i + 1, n_used - 1), 1 - slot):
            cp.start()
        prev_valid = jnp.where(i >= 1, n_valid_ref[jnp.maximum(i - 1, 0)], 0)
        for cp in scatter_copies(jnp.maximum(i - 1, 0), 1 - slot, prev_valid):
            cp.start()
        x = xbuf[slot].astype(BF16)
        gu = jnp.dot(x, wgu_ref[0], preferred_element_type=F32)
        gate, up = gu[:, :d_exp], gu[:, d_exp:]
        hmid = (gate * jax.nn.sigmoid(gate) * up).astype(BF16)
        ybuf[slot] = jnp.dot(hmid, wd_ref[0], preferred_element_type=F32)

    @pl.when(i == n_used - 1)
    def _():
        wait_gather(1 - slot)
        wait_scatter(1 - slot)
        for cp in scatter_copies(i, slot, n_valid_ref[i]):
            cp.start()
        wait_scatter(slot)


def _moe(blk_exp, n_used, n_valid, blk_base, order, x_all, wgu16, wd16):
    n_blocks = blk_exp.shape[0]
    tm = ROW_TILE
    n_tok, d = x_all.shape
    grid_spec = pltpu.PrefetchScalarGridSpec(
        num_scalar_prefetch=5,
        grid=(n_blocks,),
        in_specs=[pl.BlockSpec(memory_space=pl.ANY),
                  pl.BlockSpec((1,) + wgu16.shape[1:], lambda i, be, *_: (be[i], 0, 0)),
                  pl.BlockSpec((1,) + wd16.shape[1:], lambda i, be, *_: (be[i], 0, 0))],
        out_specs=pl.BlockSpec(memory_space=pl.ANY),
        scratch_shapes=[pltpu.VMEM((2, tm, d), F32), pltpu.VMEM((2, tm, d), F32),
                        pltpu.SemaphoreType.DMA((2,)), pltpu.SemaphoreType.DMA((2,))],
    )
    return pl.pallas_call(
        _moe_kernel,
        grid_spec=grid_spec,
        out_shape=jax.ShapeDtypeStruct((n_tok + tm, TOP_K * d), F32),
        compiler_params=_params(("arbitrary",)),
    )(blk_exp, n_used, n_valid, blk_base, order, x_all, wgu16, wd16)


def _dispatch_tables(expert_idx, n_experts, tm):
    n_tok = expert_idx.shape[0]
    n_assign = n_tok * TOP_K
    flat_e = expert_idx.reshape(-1)
    order = jnp.argsort(flat_e).astype(I32)
    experts = jnp.arange(n_experts, dtype=I32)
    counts = jnp.sum(flat_e[:, None] == experts[None, :], axis=0, dtype=I32)
    start = jnp.cumsum(counts) - counts
    blocks_of = (counts + tm - 1) // tm
    blk_end = jnp.cumsum(blocks_of)
    n_blocks = -(-n_assign // tm) + n_experts
    n_used = blk_end[-1]
    blk = jnp.minimum(jnp.arange(n_blocks, dtype=I32), n_used - 1)
    blk_exp = jnp.sum(blk_end[None, :] <= blk[:, None], axis=1, dtype=I32)
    local = blk - (blk_end - blocks_of)[blk_exp]
    blk_base = start[blk_exp] + local * tm
    n_valid = jnp.clip(counts[blk_exp] - local * tm, 0, tm)
    order = jnp.concatenate([order, jnp.zeros((tm,), I32)])
    return blk_exp, n_used.reshape(1), n_valid, blk_base, order


def _combine_rows(y_ref, r_ref, h_ref, g_ref, b_ref, alpha):
    d = h_ref.shape[1]
    r = r_ref[...]
    ffn = y_ref[:, :d] * r[:, 2:3] + y_ref[:, d:] * r[:, 3:4]
    return _layer_norm(alpha * h_ref[...] + ffn, g_ref[...], b_ref[...])


def _combine_kernel(y_ref, r_ref, h_ref, g_ref, b_ref, o_ref, *, alpha):
    o_ref[...] = _combine_rows(y_ref, r_ref, h_ref, g_ref, b_ref, alpha)


def _combine(y_pairs, r_all, h_all, g, b, alpha, first_tile, n_tiles):
    d = h_all.shape[1]
    tm = ROW_TILE
    row = lambda wd: pl.BlockSpec((tm, wd), lambda i: (i + first_tile, 0))
    const = lambda a: pl.BlockSpec(a.shape, lambda i: (0,) * a.ndim)
    return pl.pallas_call(
        functools.partial(_combine_kernel, alpha=alpha),
        grid=(n_tiles,),
        in_specs=[row(2 * d), row(LANES), row(d), const(g), const(b)],
        out_specs=pl.BlockSpec((tm, d), lambda i: (i, 0)),
        out_shape=jax.ShapeDtypeStruct((n_tiles * tm, d), F32),
        compiler_params=_params(("arbitrary",)),
    )(y_pairs, r_all, h_all, g, b)


def _combine_prompt_kernel(y_ref, r_ref, h_ref, g_ref, b_ref, out_hbm, obuf, sem, *, alpha, n_meta,
                           tiles_per_seq):
    i = pl.program_id(0)
    n_steps = pl.num_programs(0)
    tm = obuf.shape[1]
    seq = out_hbm.shape[1]
    last_rows = seq + n_meta - (tiles_per_seq - 1) * tm
    kinds = ("first", "middle", "last")

    def kind_is(j, kind):
        return {"first": j == 0, "last": j == tiles_per_seq - 1,
                "middle": jnp.logical_and(j > 0, j < tiles_per_seq - 1)}[kind]

    def tile_copy(step, kind):
        b, j, s = step // tiles_per_seq, step % tiles_per_seq, step % 2
        if kind == "first":
            src, dst = obuf.at[s, pl.ds(n_meta, tm - n_meta)], out_hbm.at[b, pl.ds(0, tm - n_meta)]
        elif kind == "last":
            src = obuf.at[s, pl.ds(0, last_rows)]
            dst = out_hbm.at[b, pl.ds((tiles_per_seq - 1) * tm - n_meta, last_rows)]
        else:
            src, dst = obuf.at[s], out_hbm.at[b, pl.ds(pl.multiple_of(j * tm - n_meta, 8), tm)]
        return pltpu.make_async_copy(src, dst, sem.at[s])

    def for_kind(step, action):
        for kind in kinds:
            @pl.when(kind_is(step % tiles_per_seq, kind))
            def _():
                action(tile_copy(step, kind))

    @pl.when(i >= 2)
    def _():
        for_kind(i - 2, lambda cp: cp.wait())

    obuf[i % 2] = _combine_rows(y_ref, r_ref, h_ref, g_ref, b_ref, alpha)
    for_kind(i, lambda cp: cp.start())

    @pl.when(i == n_steps - 1)
    def _():
        @pl.when(i >= 1)
        def _():
            for_kind(i - 1, lambda cp: cp.wait())
        for_kind(i, lambda cp: cp.wait())


def _combine_prompt(y_pairs, r_all, h_all, g, b, alpha, bsz, lp, seq, n_meta):
    d = h_all.shape[1]
    tm = PROJ_TILE
    tiles_per_seq = lp // tm
    assert lp % tm == 0 and tiles_per_seq >= 2 and n_meta % 8 == 0 and n_meta < tm
    assert 0 < seq + n_meta - (tiles_per_seq - 1) * tm <= tm and (seq + n_meta) % 8 == 0
    row = lambda wd: pl.BlockSpec((tm, wd), lambda i: (i, 0))
    const = lambda a: pl.BlockSpec(a.shape, lambda i: (0,) * a.ndim)
    return pl.pallas_call(
        functools.partial(_combine_prompt_kernel, alpha=alpha, n_meta=n_meta, tiles_per_seq=tiles_per_seq),
        grid=(bsz * tiles_per_seq,),
        in_specs=[row(2 * d), row(LANES), row(d), const(g), const(b)],
        out_specs=pl.BlockSpec(memory_space=pl.ANY),
        out_shape=jax.ShapeDtypeStruct((bsz, seq, d), F32),
        scratch_shapes=[pltpu.VMEM((2, tm, d), F32), pltpu.SemaphoreType.DMA((2,))],
        compiler_params=_params(("arbitrary",)),
    )(y_pairs, r_all, h_all, g, b)


def kernel(x_prompt, x_sample, cache_k_sb, cache_v_sb, cache_k_diff, cache_v_diff, page_table, meta_tokens, ln_in_g, ln_in_b, rel_bias, w_in, sb_norm_g, diff_lambda, diff_subln_g, w_out, ln1_g, ln1_b, w_group, b_group, w_expert_router, b_expert_router, w_gate_up, w_down, ln2_g, ln2_b):
    bsz, seq, d = x_prompt.shape
    db = x_sample.shape[0]
    assert x_sample.shape[1] == 1
    depth = w_in.shape[0]
    assert depth == 1, "the prompt/decode buffers below are laid out for a single layer"
    n_meta = meta_tokens.shape[0]
    hd = sb_norm_g.shape[-1]
    sb_kv, diff_kv = cache_k_sb.shape[3], cache_k_diff.shape[3]
    sb_w, diff_w = d // 2, d // 2
    sb_kv_w, diff_kv_w = sb_kv * hd, diff_kv * 2 * hd
    splits = [0, sb_w, sb_w + sb_kv_w, sb_w + 2 * sb_kv_w, sb_w + 2 * sb_kv_w + diff_w,
              sb_w + 2 * sb_kv_w + diff_w + diff_kv_w, sb_w + 2 * sb_kv_w + diff_w + 2 * diff_kv_w]
    assert splits[-1] == w_in.shape[-1]
    n_groups = w_group.shape[-1]
    n_experts = w_expert_router.shape[-1]
    per_group = n_experts // n_groups
    page = cache_k_sb.shape[2]
    past = page_table.shape[1] * page
    alpha = (2 * depth) ** 0.25
    li = 0
    lam_init = 0.8 - 0.6 * math.exp(-0.3 * li)

    seq_len = n_meta + seq
    lp = -(-seq_len // (LANES * PROJ_TILE // math.gcd(LANES, PROJ_TILE))) * (LANES * PROJ_TILE // math.gcd(LANES, PROJ_TILE))
    xpad = jnp.concatenate([jnp.broadcast_to(meta_tokens[None], (bsz, n_meta, d)), x_prompt,
                            jnp.zeros((bsz, lp - seq_len, d), F32)], axis=1)
    g_in, b_in = ln_in_g.reshape(1, d), ln_in_b.reshape(1, d)
    w_in16 = w_in[li].astype(BF16)
    w_out16 = w_out[li].astype(BF16)
    wgu16 = w_gate_up[li].astype(BF16)
    wd16 = w_down[li].astype(BF16)
    w_router = jnp.zeros((d, LANES), F32).at[:, :n_groups].set(w_group[li]) \
        .at[:, n_groups:n_groups + n_experts].set(w_expert_router[li])
    b_router = jnp.zeros((1, LANES), F32).at[0, :n_groups].set(b_group[li]) \
        .at[0, n_groups:n_groups + n_experts].set(b_expert_router[li])
    w_router = w_router.astype(BF16)
    g1, b1 = ln1_g[li].reshape(1, d), ln1_b[li].reshape(1, d)
    g2, b2 = ln2_g[li].reshape(1, d), ln2_b[li].reshape(1, d)
    g_sb = sb_norm_g[li].reshape(1, hd)
    g_diff = diff_subln_g[li].reshape(1, 2 * hd)
    lam_params = diff_lambda[li]
    idx = jnp.arange(MXU_DIM)
    tri_half = (idx[:LANES, None] >= idx[None, :LANES])
    tri = jnp.concatenate([tri_half, jnp.ones((LANES, LANES), bool)], axis=1).astype(BF16)
    tri = jnp.concatenate([tri, tri], axis=0)

    near, far, dec = _bias_tables(rel_bias, past)

    (hp, qsb16, qd16, ksb16, vsb16, kd16, vd16, k_sb, v_sb, k_d, v_d) = _ln_proj_prompt(
        xpad, g_in, b_in, w_in16, splits, seq_len, sb_kv, diff_kv)
    o_sb = _sb_prompt(qsb16, ksb16, vsb16, tri, g_sb, sb_kv)
    o_d = _diff_prompt(qd16, kd16, vd16, near, far, lam_params, g_diff, diff_kv, lam_init)
    n_prompt = bsz * lp
    total_rows = n_prompt + ROW_TILE
    assert n_prompt % ROW_TILE == 0 and db <= ROW_TILE
    h1_all, r_all = _merge(o_sb.reshape(n_prompt, sb_w), o_d.reshape(n_prompt, diff_w),
                           hp.reshape(n_prompt, d), w_out16, g1, b1, w_router, b_router, total_rows, 0,
                           None, alpha, n_groups, per_group)

    hs, proj_s = _ln_proj_rows(x_sample.reshape(db, d), g_in, b_in, w_in16)
    qn_sb, kn_sb, vn_sb, qn_d, kn_d, vn_d = [proj_s[:, splits[t]:splits[t + 1]] for t in range(6)]
    pt_flat = page_table.reshape(-1).astype(I32)
    os_sb = _sb_decode(pt_flat, qn_sb, tri, g_sb, cache_k_sb, cache_v_sb, li)
    dec_heads = dec[:, 0, :].reshape(diff_kv, -1, past)
    dec_rows = jnp.concatenate([dec_heads] * 2, axis=1)
    bias0 = jnp.concatenate([(rel_bias[0] * LOG2E).reshape(diff_kv, -1, 1)] * 2, axis=1)
    os_d = _diff_decode(pt_flat, qn_d, kn_d, vn_d, dec_rows, bias0, lam_params, g_diff, cache_k_diff,
                        cache_v_diff, li, lam_init)
    pad_rows = lambda a: jnp.concatenate([a, jnp.zeros((ROW_TILE - db, a.shape[1]), a.dtype)], axis=0)
    h1_all, r_all = _merge(pad_rows(os_sb), pad_rows(os_d), pad_rows(hs), w_out16, g1, b1, w_router,
                           b_router, total_rows, n_prompt // ROW_TILE, (h1_all, r_all), alpha, n_groups,
                           per_group)

    expert_idx = r_all[:, :TOP_K].astype(I32)
    tables = _dispatch_tables(expert_idx, n_experts, ROW_TILE)
    y_pairs = _moe(*tables, h1_all, wgu16, wd16)
    y_prompt = _combine_prompt(y_pairs, r_all, h1_all, g2, b2, alpha, bsz, lp, seq, n_meta)
    y_sample = _combine(y_pairs, r_all, h1_all, g2, b2, alpha, n_prompt // ROW_TILE, 1)[:db].reshape(db, 1, d)
    return (y_prompt, y_sample,
            k_sb[None], v_sb[None], k_d[None], v_d[None],
            kn_sb.reshape(1, db, 1, sb_kv, hd), vn_sb.reshape(1, db, 1, sb_kv, hd),
            kn_d.reshape(1, db, 1, diff_kv, 2 * hd), vn_d.reshape(1, db, 1, diff_kv, 2 * hd))
```

```python
import functools
import math

import jax
import jax.numpy as jnp
from jax import lax
from jax.experimental import pallas as pl
from jax.experimental.pallas import tpu as pltpu

F32 = jnp.float32
BF16 = jnp.bfloat16
I32 = jnp.int32

LANES = 128
MXU_DIM = 256
LN_EPS = 1e-5
REL_BUCKETS, REL_MAX_EXACT, REL_MAX_DIST = 32, 16, 128
TOP_K = 2
TOP_K_SHIFT = TOP_K.bit_length() - 1
LOG2E = math.log2(math.e)
MASKED = -1e30
SB_UNDERFLOW = -104.0
ROW_TILE = 256
PROJ_TILE = 320
DEC_PAGES = 32
FAR_TILES = 8
VMEM_LIMIT = 56 * 1024 * 1024

NT_DIMS = (((1,), (1,)), ((), ()))


def _params(sem, vmem=VMEM_LIMIT):
    return pltpu.CompilerParams(dimension_semantics=sem, vmem_limit_bytes=vmem)


def _layer_norm(x, g, b):
    mu = jnp.mean(x, axis=-1, keepdims=True)
    xc = x - mu
    var = jnp.mean(xc * xc, axis=-1, keepdims=True)
    return xc * lax.rsqrt(var + LN_EPS) * g + b


def _rms_norm(x, g):
    return x * lax.rsqrt(jnp.mean(x * x, axis=-1, keepdims=True) + LN_EPS) * g


def _split_bf16(x):
    hi = x.astype(BF16)
    lo = (x - hi.astype(F32)).astype(BF16)
    return hi, lo


def _lambda_value(lp, lam_init):
    a = jnp.sum(lp[0:1] * lp[1:2], axis=-1, keepdims=True)
    b = jnp.sum(lp[2:3] * lp[3:4], axis=-1, keepdims=True)
    return jnp.exp(a) - jnp.exp(b) + lam_init


def _ln_proj_prompt_kernel(x_ref, g_ref, b_ref, w_ref, h_ref, qsb_ref, qd_ref,
                           ksb16_ref, vsb16_ref, kd16_ref, vd16_ref,
                           ksb_ref, vsb_ref, kd_ref, vd_ref, *, splits):
    h = _layer_norm(x_ref[0], g_ref[...], b_ref[...])
    h_ref[0] = h
    hb = h.astype(BF16)

    def proj(seg):
        lo, hi = splits[seg], splits[seg + 1]
        return jnp.dot(hb, w_ref[:, lo:hi], preferred_element_type=F32)

    def emit(seg, full_ref, half_ref):
        y = proj(seg)
        width = full_ref.shape[3]
        for h in range(full_ref.shape[2]):
            full_ref[0, :, h, :] = y[:, h * width:(h + 1) * width]
        half_ref[0] = y.astype(BF16)

    qsb_ref[0] = proj(0).astype(BF16)
    emit(1, ksb_ref, ksb16_ref)
    emit(2, vsb_ref, vsb16_ref)
    qd_ref[0] = proj(3).astype(BF16)
    emit(4, kd_ref, kd16_ref)
    emit(5, vd_ref, vd16_ref)


def _ln_proj_prompt(xpad, g, b, w16, splits, seq_len, sb_kv, diff_kv):
    bsz, lp, d = xpad.shape
    widths = [splits[i + 1] - splits[i] for i in range(6)]
    tm = PROJ_TILE
    assert lp % tm == 0 and lp - seq_len < tm
    row = lambda wd: pl.BlockSpec((1, tm, wd), lambda bi, i: (bi, i, 0))
    heads = lambda n, wd: pl.BlockSpec((1, tm, n, wd // n), lambda bi, i: (bi, i, 0, 0))
    const = lambda shp: pl.BlockSpec(shp, lambda bi, i: (0,) * len(shp))
    out_shape = (
        jax.ShapeDtypeStruct((bsz, lp, d), F32),
        jax.ShapeDtypeStruct((bsz, lp, widths[0]), BF16),
        jax.ShapeDtypeStruct((bsz, lp, widths[3]), BF16),
        jax.ShapeDtypeStruct((bsz, lp, widths[1]), BF16),
        jax.ShapeDtypeStruct((bsz, lp, widths[2]), BF16),
        jax.ShapeDtypeStruct((bsz, lp, widths[4]), BF16),
        jax.ShapeDtypeStruct((bsz, lp, widths[5]), BF16),
        jax.ShapeDtypeStruct((bsz, seq_len, sb_kv, widths[1] // sb_kv), F32),
        jax.ShapeDtypeStruct((bsz, seq_len, sb_kv, widths[2] // sb_kv), F32),
        jax.ShapeDtypeStruct((bsz, seq_len, diff_kv, widths[4] // diff_kv), F32),
        jax.ShapeDtypeStruct((bsz, seq_len, diff_kv, widths[5] // diff_kv), F32),
    )
    out_specs = (row(d), row(widths[0]), row(widths[3]), row(widths[1]), row(widths[2]),
                 row(widths[4]), row(widths[5]), heads(sb_kv, widths[1]), heads(sb_kv, widths[2]),
                 heads(diff_kv, widths[4]), heads(diff_kv, widths[5]))
    return pl.pallas_call(
        functools.partial(_ln_proj_prompt_kernel, splits=tuple(splits)),
        grid=(bsz, lp // tm),
        in_specs=[row(d), const((1, d)), const((1, d)),
                  pl.BlockSpec(w16.shape, lambda bi, i: (0, 0), pipeline_mode=pl.Buffered(1))],
        out_specs=out_specs,
        out_shape=out_shape,
        compiler_params=_params(("arbitrary", "arbitrary")),
    )(xpad, g, b, w16)


def _ln_proj_rows_kernel(x_ref, g_ref, b_ref, w_ref, h_ref, p_ref):
    h = _layer_norm(x_ref[...], g_ref[...], b_ref[...])
    h_ref[...] = h
    p_ref[...] = jnp.dot(h.astype(BF16), w_ref[...], preferred_element_type=F32)


def _ln_proj_rows(x, g, b, w):
    n, d = x.shape
    pw = w.shape[1]
    tn = 512
    assert pw % tn == 0
    return pl.pallas_call(
        _ln_proj_rows_kernel,
        grid=(pw // tn,),
        in_specs=[pl.BlockSpec((n, d), lambda j: (0, 0)), pl.BlockSpec((1, d), lambda j: (0, 0)),
                  pl.BlockSpec((1, d), lambda j: (0, 0)), pl.BlockSpec((d, tn), lambda j: (0, j))],
        out_specs=(pl.BlockSpec((n, d), lambda j: (0, 0)), pl.BlockSpec((n, tn), lambda j: (0, j))),
        out_shape=(jax.ShapeDtypeStruct((n, d), F32), jax.ShapeDtypeStruct((n, pw), F32)),
        compiler_params=_params(("arbitrary",)),
    )(x, g, b, w)


def _bias_from_dist(dist, rel_ref, head):
    n = jnp.maximum(dist, 0)
    nf = jnp.maximum(n, 1).astype(F32)
    large = REL_MAX_EXACT + (jnp.log(nf / REL_MAX_EXACT) / math.log(REL_MAX_DIST / REL_MAX_EXACT)
                             * (REL_BUCKETS - REL_MAX_EXACT)).astype(I32)
    large = jnp.minimum(large, REL_BUCKETS - 1)
    bucket = jnp.where(n < REL_MAX_EXACT, n, large)
    out = jnp.zeros(dist.shape, F32)
    for bkt in range(REL_BUCKETS):
        out = jnp.where(bucket == bkt, rel_ref[bkt, head], out)
    return out * LOG2E


def _bias_tables_kernel(rel_ref, near_ref, far_ref, dec_ref, *, n_heads, past):
    i = lax.broadcasted_iota(I32, (LANES, 2 * LANES), 0)
    j = lax.broadcasted_iota(I32, (LANES, 2 * LANES), 1)
    dist = i + LANES - j
    kpos = lax.broadcasted_iota(I32, dec_ref.shape[1:], 1)
    for h in range(n_heads):
        near_ref[h] = jnp.where(dist >= 0, _bias_from_dist(dist, rel_ref, h), MASKED)
        far_ref[h] = jnp.full(far_ref.shape[1:], rel_ref[REL_BUCKETS - 1, h] * LOG2E, F32)
        dec_ref[h] = _bias_from_dist(past - kpos, rel_ref, h)


def _bias_tables(rel_bias, past):
    n_heads = rel_bias.shape[1]
    return pl.pallas_call(
        functools.partial(_bias_tables_kernel, n_heads=n_heads, past=past),
        in_specs=[pl.BlockSpec(memory_space=pltpu.SMEM)],
        out_shape=(jax.ShapeDtypeStruct((n_heads, LANES, 2 * LANES), F32),
                   jax.ShapeDtypeStruct((n_heads, LANES, LANES), F32),
                   jax.ShapeDtypeStruct((n_heads, 8, past), F32)),
    )(rel_bias)


def _sb_tile(z, vis, tri, carry):
    l = -(jnp.maximum(z, 0.0) + jnp.log(1.0 + jnp.exp(-jnp.abs(z))))
    if vis is not None:
        l = jnp.where(vis, l, 0.0)
    hi, lo = _split_bf16(l)
    t2 = jnp.dot(jnp.concatenate([hi, lo], axis=1), tri, preferred_element_type=F32)
    nt = z.shape[1]
    incl, total = t2[:, :nt], t2[:, nt:]
    w = jnp.exp(z + incl + carry)
    if vis is not None:
        w = jnp.where(vis, w, 0.0)
    return w, carry + total


def _sb_prompt_kernel(q_ref, k_ref, v_ref, tri_ref, g_ref, o_ref, acc_ref, carry_ref, *, scale):
    qi = pl.program_id(0)
    bsz, n_kv, m, hd = acc_ref.shape
    group = m // LANES
    qh = [[jnp.concatenate([q_ref[b, :, (h * group + g) * hd:(h * group + g + 1) * hd]
                            for g in range(group)], axis=0) for h in range(n_kv)] for b in range(bsz)]
    acc_ref[...] = jnp.zeros_like(acc_ref)
    carry_ref[...] = jnp.zeros_like(carry_ref)
    chains = [(b, h) for b in range(bsz) for h in range(n_kv)]
    n_rows = len(chains) * m
    rowpos = lax.broadcasted_iota(I32, (n_rows, LANES), 0) & (LANES - 1)
    col = lax.broadcasted_iota(I32, (n_rows, LANES), 1)
    tri = tri_ref[...]

    def cond(state):
        j, live = state
        return jnp.logical_and(j >= 0, live > SB_UNDERFLOW)

    def body(state):
        j, _ = state
        ks = pl.multiple_of(j * LANES, LANES)
        z = jnp.concatenate(
            [lax.dot_general(qh[b][h], k_ref[b, pl.ds(ks, LANES), h * hd:(h + 1) * hd], NT_DIMS,
                             preferred_element_type=F32) for b, h in chains], axis=0) * scale
        vis = col < rowpos + (qi - j) * LANES
        w, carry = _sb_tile(z, vis, tri, carry_ref[...])
        carry_ref[...] = carry
        w = w.astype(BF16)
        for n, (b, h) in enumerate(chains):
            acc_ref[b, h] += jnp.dot(w[n * m:(n + 1) * m], v_ref[b, pl.ds(ks, LANES), h * hd:(h + 1) * hd],
                                     preferred_element_type=F32)
        return j - 1, jnp.max(carry)

    lax.while_loop(cond, body, (qi, jnp.float32(0.0)))
    for b in range(bsz):
        outs = []
        for h in range(n_kv):
            o = _rms_norm(acc_ref[b, h], g_ref[...]).astype(o_ref.dtype)
            outs += [o[g * LANES:(g + 1) * LANES] for g in range(group)]
        o_ref[b] = jnp.concatenate(outs, axis=1)


def _sb_prompt(q16, k16, v16, tri, g, n_kv):
    bsz, lp, width = q16.shape
    kvw = k16.shape[2]
    hd = kvw // n_kv
    m = width // n_kv // hd * LANES
    resident = lambda: pl.BlockSpec((bsz, lp, kvw), lambda i: (0, 0, 0), pipeline_mode=pl.Buffered(1))
    return pl.pallas_call(
        functools.partial(_sb_prompt_kernel, scale=hd ** -0.5),
        grid=(lp // LANES,),
        in_specs=[pl.BlockSpec((bsz, LANES, width), lambda i: (0, i, 0)),
                  resident(), resident(),
                  pl.BlockSpec(tri.shape, lambda i: (0, 0)),
                  pl.BlockSpec((1, hd), lambda i: (0, 0))],
        out_specs=pl.BlockSpec((bsz, LANES, width), lambda i: (0, i, 0)),
        out_shape=jax.ShapeDtypeStruct((bsz, lp, width), BF16),
        scratch_shapes=[pltpu.VMEM((bsz, n_kv, m, hd), F32), pltpu.VMEM((bsz * n_kv * m, LANES), F32)],
        compiler_params=_params(("arbitrary",)),
    )(q16, k16, v16, tri, g)


def _sb_decode_kernel(pt_ref, q_ref, tri_ref, g_ref, k_hbm, v_hbm, o_ref, kbuf, vbuf, sem, *,
                      scale, n_kv, n_pages, layer):
    s = pl.program_id(0)
    q = q_ref[0]
    hd = g_ref.shape[1]
    n_heads = q.shape[1] // hd
    group = n_heads // n_kv
    zeros = jnp.zeros((1, hd), F32)
    rows = []
    for n in range(n_heads):
        seg = q[:, n * hd:(n + 1) * hd]
        rows.append(jnp.concatenate([seg if h == n // group else zeros for h in range(n_kv)], axis=1))
    qb = jnp.concatenate(rows, axis=0)
    q16 = qb.astype(BF16)
    tri = tri_ref[...]

    def page_copies(j, slot):
        page = pt_ref[s * n_pages + j]
        return [pltpu.make_async_copy(hbm.at[layer, page, :, h, :], buf.at[slot, h], sem.at[which, slot])
                for which, (hbm, buf) in enumerate(((k_hbm, kbuf), (v_hbm, vbuf))) for h in range(n_kv)]

    def fetch(j, slot):
        for cp in page_copies(j, slot):
            cp.start()

    def wait_page(j, slot):
        for cp in page_copies(j, slot):
            cp.wait()

    def heads_on_lanes(buf, slot):
        return jnp.concatenate([buf[slot, h] for h in range(n_kv)], axis=1)

    fetch(n_pages - 1, 0)

    def cond(state):
        j, live, _, _ = state
        return jnp.logical_and(j >= 0, live > SB_UNDERFLOW)

    def body(state):
        j, _, carry, acc = state
        slot = (n_pages - 1 - j) % 2
        wait_page(j, slot)

        @pl.when(j > 0)
        def _():
            fetch(j - 1, 1 - slot)

        k = heads_on_lanes(kbuf, slot).astype(BF16)
        v = heads_on_lanes(vbuf, slot).astype(BF16)
        z = lax.dot_general(q16, k, NT_DIMS, preferred_element_type=F32) * scale
        w, carry = _sb_tile(z, None, tri, carry)
        acc = acc + jnp.dot(w.astype(BF16), v, preferred_element_type=F32)
        return j - 1, jnp.max(carry), carry, acc

    init = (jnp.int32(n_pages - 1), jnp.float32(0.0), jnp.zeros((n_heads, LANES), F32),
            jnp.zeros((n_heads, n_kv * hd), F32))
    j_end, _, _, acc = lax.while_loop(cond, body, init)

    @pl.when(j_end >= 0)
    def _():
        wait_page(j_end, (n_pages - 1 - j_end) % 2)

    outs = []
    for n in range(n_heads):
        h = n // group
        outs.append(_rms_norm(acc[n:n + 1, h * hd:(h + 1) * hd], g_ref[...]))
    o_ref[0] = jnp.concatenate(outs, axis=1)


def _sb_decode(pt_flat, q, tri, g, cache_k, cache_v, layer):
    db, width = q.shape
    depth, n_pool, ps, n_kv, hd = cache_k.shape
    assert ps == LANES
    n_pages = pt_flat.shape[0] // db
    grid_spec = pltpu.PrefetchScalarGridSpec(
        num_scalar_prefetch=1,
        grid=(db,),
        in_specs=[pl.BlockSpec((1, 1, width), lambda s, pt: (s, 0, 0)),
                  pl.BlockSpec(tri.shape, lambda s, pt: (0, 0)),
                  pl.BlockSpec((1, hd), lambda s, pt: (0, 0)),
                  pl.BlockSpec(memory_space=pl.ANY), pl.BlockSpec(memory_space=pl.ANY)],
        out_specs=pl.BlockSpec((1, 1, width), lambda s, pt: (s, 0, 0)),
        scratch_shapes=[pltpu.VMEM((2, n_kv, ps, hd), F32), pltpu.VMEM((2, n_kv, ps, hd), F32),
                        pltpu.SemaphoreType.DMA((2, 2))],
    )
    out = pl.pallas_call(
        functools.partial(_sb_decode_kernel, scale=hd ** -0.5, n_kv=n_kv, n_pages=n_pages, layer=layer),
        grid_spec=grid_spec,
        out_shape=jax.ShapeDtypeStruct((db, 1, width), F32),
        compiler_params=_params(("arbitrary",)),
    )(pt_flat, q.reshape(db, 1, width), tri, g, cache_k, cache_v)
    return out.reshape(db, width)


def _diff_prompt_kernel(q_ref, k_ref, v_ref, near_ref, far_ref, lam_ref, g_ref, o_ref,
                        m_ref, l_ref, acc_ref, *, scale, lam_init, far_tiles):
    qi = pl.program_id(1)
    n_kv = m_ref.shape[0]
    kw = k_ref.shape[2] // n_kv
    hd = kw // 2
    group = q_ref.shape[2] // (n_kv * kw)
    rows = group * LANES
    qm = [[jnp.concatenate([q_ref[0, :, ((h * group + g) * 2 + c) * hd:((h * group + g) * 2 + c + 1) * hd]
                            for g in range(group)], axis=0) for c in range(2)] for h in range(n_kv)]
    c1 = scale * LOG2E

    def scores(h, c, ks, width):
        k = k_ref[0, pl.ds(ks, width), h * kw + c * hd:h * kw + (c + 1) * hd]
        return lax.dot_general(qm[h][c], k, NT_DIMS, preferred_element_type=F32) * c1

    def lane_tiles(x):
        return [x[:, t * LANES:(t + 1) * LANES] for t in range(x.shape[1] // LANES)]

    def row_max(s):
        return jnp.max(functools.reduce(jnp.maximum, lane_tiles(s)), axis=-1, keepdims=True)

    def exp_tiles(s, shift):
        ps = [jnp.exp2(t - shift) for t in lane_tiles(s)]
        return jnp.concatenate(ps, axis=1).astype(BF16), functools.reduce(jnp.add, ps)

    head_maps = [(h, c) for h in range(n_kv) for c in range(2)]

    def near_chunk(ks, first_col, width):
        ss = [scores(h, c, ks, width) + near_ref[h, :, first_col:first_col + width] for h, c in head_maps]
        ps = []
        for (h, c), s in zip(head_maps, ss):
            mx = jnp.broadcast_to(row_max(s), (rows, LANES))
            p, psum = exp_tiles(s, mx)
            m_ref[h, 0, c] = mx
            l_ref[h, 0, c] = psum
            ps.append(p)
        for (h, c), p in zip(head_maps, ps):
            acc_ref[h, 0, c] = jnp.dot(p, v_ref[0, pl.ds(ks, width), h * kw:(h + 1) * kw],
                                       preferred_element_type=F32)

    @pl.when(qi == 0)
    def _():
        near_chunk(0, LANES, LANES)

    @pl.when(qi > 0)
    def _():
        near_chunk(pl.multiple_of((qi - 1) * LANES, LANES), 0, 2 * LANES)

    for h in range(n_kv):
        m_ref[h, 1] = jnp.full(m_ref.shape[2:], MASKED, F32)
        l_ref[h, 1] = jnp.zeros(l_ref.shape[2:], F32)
        acc_ref[h, 1] = jnp.zeros(acc_ref.shape[2:], F32)

    def far_chunks(chunks):
        for ks, stream, keep in chunks:
            for h, c in head_maps:
                s = scores(h, c, ks, wide)
                if keep is not None:
                    kpos = ks + lax.broadcasted_iota(I32, s.shape, 1)
                    s = jnp.where(jnp.logical_and(kpos >= keep[0], kpos < keep[1]), s, MASKED)
                fb = far_ref[h]
                m_old = m_ref[h, stream, c]
                m_new = jnp.maximum(m_old, row_max(s) + fb)
                p, psum = exp_tiles(s, m_new - fb)
                pv = jnp.dot(p, v_ref[0, pl.ds(ks, wide), h * kw:(h + 1) * kw], preferred_element_type=F32)
                alpha = jnp.exp2(m_old - m_new)
                l_ref[h, stream, c] = alpha * l_ref[h, stream, c] + psum
                acc_ref[h, stream, c] = (jnp.concatenate([alpha] * (kw // LANES), axis=1) * acc_ref[h, stream, c]
                                         + pv)
                m_ref[h, stream, c] = m_new

    n_far = jnp.maximum(qi - 1, 0)
    wide = far_tiles * LANES
    n_full = n_far // far_tiles

    def pair_body(t, carry):
        far_chunks([(pl.multiple_of(2 * t * wide, wide), 0, None)])
        far_chunks([(pl.multiple_of((2 * t + 1) * wide, wide), 1, None)])
        return carry

    lax.fori_loop(0, n_full // 2, pair_body, 0)

    @pl.when(n_full % 2 == 1)
    def _():
        far_chunks([(pl.multiple_of((n_full - 1) * wide, wide), 0, None)])

    @pl.when(n_far % far_tiles != 0)
    def _():
        ks = pl.multiple_of(jnp.minimum(n_full * wide, k_ref.shape[1] - wide), LANES)
        far_chunks([(ks, 1, (n_full * wide, n_far * LANES))])

    lam = _lambda_value(lam_ref[...], lam_init)
    outs = []
    for h in range(n_kv):
        merged = []
        for c in range(2):
            m = jnp.maximum(m_ref[h, 0, c], m_ref[h, 1, c])
            a0, a1 = jnp.exp2(m_ref[h, 0, c] - m), jnp.exp2(m_ref[h, 1, c] - m)
            l = jnp.sum(a0 * l_ref[h, 0, c] + a1 * l_ref[h, 1, c], axis=-1, keepdims=True)
            merged.append((a0[:, :1] * acc_ref[h, 0, c] + a1[:, :1] * acc_ref[h, 1, c]) / l)
        a = merged[0] - lam * merged[1]
        o = _rms_norm(a, g_ref[...] * (1.0 - lam_init)).astype(o_ref.dtype)
        outs += [o[g * LANES:(g + 1) * LANES] for g in range(group)]
    o_ref[0] = jnp.concatenate(outs, axis=1)


def _diff_prompt(q16, k16, v16, near, far, lam_params, g, n_kv, lam_init):
    bsz, lp, width = q16.shape
    kvw = k16.shape[2]
    kw = kvw // n_kv
    group = width // kvw
    rows = group * LANES
    nq = lp // LANES
    far_tiles = min(FAR_TILES, nq)
    near_g = near.reshape(n_kv, rows, 2 * LANES)
    far_g = far.reshape(n_kv, rows, LANES)
    resident = lambda: pl.BlockSpec((1, lp, kvw), lambda b, i: (b, 0, 0), pipeline_mode=pl.Buffered(1))
    const = lambda a: pl.BlockSpec(a.shape, lambda b, i: (0,) * a.ndim)
    return pl.pallas_call(
        functools.partial(_diff_prompt_kernel, scale=(kw // 2) ** -0.5, lam_init=lam_init,
                          far_tiles=far_tiles),
        grid=(bsz, nq),
        in_specs=[pl.BlockSpec((1, LANES, width), lambda b, i: (b, i, 0)), resident(), resident(),
                  const(near_g), const(far_g), const(lam_params), const(g)],
        out_specs=pl.BlockSpec((1, LANES, width), lambda b, i: (b, i, 0)),
        out_shape=jax.ShapeDtypeStruct((bsz, lp, width), BF16),
        scratch_shapes=[pltpu.VMEM((n_kv, 2, 2, rows, LANES), F32), pltpu.VMEM((n_kv, 2, 2, rows, LANES), F32),
                        pltpu.VMEM((n_kv, 2, 2, rows, kw), F32)],
        compiler_params=_params(("arbitrary",) * 2),
    )(q16, k16, v16, near_g, far_g, lam_params, g)


def _diff_decode_kernel(pt_ref, q_ref, kn_ref, vn_ref, bias_ref, bias0_ref, lam_ref, g_ref, k_hbm, v_hbm,
                        o_ref, kbuf, vbuf, sem, s_ref, snew_ref, m_ref, a_ref, anew_ref, acc_ref, *,
                        scale, lam_init, n_pages, pages, layer, n_seq):
    s = pl.program_id(0)
    c = pl.program_id(1)
    nc = n_pages // pages
    n_kv, span, kw = kbuf.shape[1:]
    ps = span // pages
    hd = kw // 2
    group = q_ref.shape[2] // (n_kv * kw)
    step = s * nc + c
    slot = step % 2
    c1 = scale * LOG2E

    def chunk_copies(step_idx, dst_slot):
        seq, ch = step_idx // nc, step_idx % nc
        base_k = jnp.minimum(seq, n_seq - 1) * n_pages + ch * pages
        base_v = jnp.maximum(seq - 1, 0) * n_pages + ch * pages
        out = []
        for t in range(pages):
            for which, (hbm, buf, base) in enumerate(((k_hbm, kbuf, base_k), (v_hbm, vbuf, base_v))):
                page = pt_ref[base + t]
                out += [pltpu.make_async_copy(hbm.at[layer, page, :, h, :],
                                              buf.at[dst_slot, h, pl.ds(t * ps, ps)], sem.at[which, dst_slot])
                        for h in range(n_kv)]
        return out

    @pl.when(step == 0)
    def _():
        for cp in chunk_copies(step, slot):
            cp.start()

    @pl.when(step + 1 < (n_seq + 1) * nc)
    def _():
        for cp in chunk_copies(step + 1, 1 - slot):
            cp.start()

    for cp in chunk_copies(step, slot):
        cp.wait()

    lam = _lambda_value(lam_ref[...], lam_init)
    cols = pl.ds(pl.multiple_of(c * span, span), span)

    @pl.when(s < n_seq)
    def _():
        q = q_ref[0]
        zeros = jnp.zeros((1, hd), F32)
        for h in range(n_kv):
            rows = []
            for mp in range(2):
                for g in range(group):
                    lo = ((h * group + g) * 2 + mp) * hd
                    rows.append(jnp.concatenate([q[:, lo:lo + hd], zeros] if mp == 0 else
                                                [zeros, q[:, lo:lo + hd]], axis=1))
            qb = jnp.concatenate(rows, axis=0).astype(BF16)

            @pl.when(c == 0)
            def _():
                kn = kn_ref[0][:, h * kw:(h + 1) * kw].astype(BF16)
                s0 = jnp.sum(qb.astype(F32) * kn.astype(F32), axis=-1, keepdims=True) * c1 + bias0_ref[h]
                snew_ref[h] = jnp.broadcast_to(s0, snew_ref.shape[1:])
                m_ref[h] = jnp.broadcast_to(s0, m_ref.shape[1:])

            sc = lax.dot_general(qb, kbuf[slot, h].astype(BF16), NT_DIMS,
                                 preferred_element_type=F32) * c1 + bias_ref[h]
            s_ref[h, :, cols] = sc
            m_ref[h] = jnp.maximum(m_ref[h], jnp.max(sc, axis=-1, keepdims=True))

            @pl.when(c == nc - 1)
            def _():
                m = m_ref[h][:, :1]
                p = jnp.exp2(s_ref[h] - m)
                p0 = jnp.exp2(snew_ref[h][:, :1] - m)
                l = jnp.sum(p, axis=-1, keepdims=True) + p0
                pn, pn0 = p / l, p0 / l
                a_ref[s % 2, h] = (pn[:group] - lam * pn[group:]).astype(a_ref.dtype)
                anew_ref[s % 2, h] = jnp.broadcast_to(pn0[:group] - lam * pn0[group:], anew_ref.shape[2:])

    @pl.when(s >= 1)
    def _():
        par = (s - 1) % 2
        for h in range(n_kv):
            @pl.when(c == 0)
            def _():
                a0 = anew_ref[par, h][:, :1].astype(BF16).astype(F32)
                vn = vn_ref[0][:, h * kw:(h + 1) * kw].astype(BF16).astype(F32)
                acc_ref[h] = a0 * vn

            acc_ref[h] += jnp.dot(a_ref[par, h, :, cols], vbuf[slot, h].astype(BF16),
                                  preferred_element_type=F32)

            @pl.when(c == nc - 1)
            def _():
                gain = g_ref[...] * (1.0 - lam_init)
                for g in range(group):
                    lo = (h * group + g) * kw
                    o_ref[0, :, lo:lo + kw] = _rms_norm(acc_ref[h][g:g + 1], gain)


def _diff_decode(pt_flat, q, k_new, v_new, dec_bias, bias0, lam_params, g, cache_k, cache_v, layer,
                 lam_init):
    db, width = q.shape
    depth, n_pool, ps, n_kv, kw = cache_k.shape
    n_pages = pt_flat.shape[0] // db
    pages = min(DEC_PAGES, n_pages)
    assert n_pages % pages == 0
    group = width // (n_kv * kw)
    n_rows = 2 * group
    span = pages * ps
    past = n_pages * ps

    cur = lambda wd: pl.BlockSpec((1, 1, wd), lambda s, c, pt: (jnp.minimum(s, db - 1), 0, 0))
    prev = lambda wd: pl.BlockSpec((1, 1, wd), lambda s, c, pt: (jnp.maximum(s - 1, 0), 0, 0))
    const = lambda shp: pl.BlockSpec(shp, lambda s, c, pt: (0,) * len(shp))
    grid_spec = pltpu.PrefetchScalarGridSpec(
        num_scalar_prefetch=1,
        grid=(db + 1, n_pages // pages),
        in_specs=[cur(width), cur(n_kv * kw), prev(n_kv * kw),
                  pl.BlockSpec((n_kv, n_rows, span), lambda s, c, pt: (0, 0, c)),
                  const((n_kv, n_rows, 1)), const(lam_params.shape), const((1, kw)),
                  pl.BlockSpec(memory_space=pl.ANY), pl.BlockSpec(memory_space=pl.ANY)],
        out_specs=prev(width),
        scratch_shapes=[pltpu.VMEM((2, n_kv, span, kw), F32),
                        pltpu.VMEM((2, n_kv, span, kw), F32),
                        pltpu.SemaphoreType.DMA((2, 2)),
                        pltpu.VMEM((n_kv, n_rows, past), F32),
                        pltpu.VMEM((n_kv, n_rows, LANES), F32),
                        pltpu.VMEM((n_kv, n_rows, LANES), F32),
                        pltpu.VMEM((2, n_kv, group, past), BF16),
                        pltpu.VMEM((2, n_kv, group, LANES), F32),
                        pltpu.VMEM((n_kv, group, kw), F32)],
    )
    out = pl.pallas_call(
        functools.partial(_diff_decode_kernel, scale=(kw // 2) ** -0.5, lam_init=lam_init,
                          n_pages=n_pages, pages=pages, layer=layer, n_seq=db),
        grid_spec=grid_spec,
        out_shape=jax.ShapeDtypeStruct((db, 1, width), F32),
        compiler_params=_params(("arbitrary",) * 2),
    )(pt_flat, q.reshape(db, 1, width), k_new.reshape(db, 1, n_kv * kw), v_new.reshape(db, 1, n_kv * kw),
      dec_bias, bias0, lam_params, g, cache_k, cache_v)
    return out.reshape(db, width)


def _route(logits, bias, n_groups, per_group):
    x = logits + bias
    lane = lax.broadcasted_iota(I32, x.shape, 1).astype(F32)
    big = jnp.float32(4 * LANES)
    neg = jnp.float32(-jnp.inf)

    def first_argmax(vals, mask):
        mx = jnp.max(jnp.where(mask, vals, neg), axis=-1, keepdims=True)
        idx = jnp.min(jnp.where(jnp.logical_and(mask, vals == mx), lane, big), axis=-1, keepdims=True)
        return mx, idx

    gmask = lane < n_groups
    gmax, grp = first_argmax(x, gmask)
    p_sel = 1.0 / jnp.sum(jnp.where(gmask, jnp.exp(x - gmax), 0.0), axis=-1, keepdims=True)
    emask = jnp.logical_and(lane >= n_groups + grp * per_group, lane < n_groups + (grp + 1) * per_group)
    v1, i1 = first_argmax(x, emask)
    v2, i2 = first_argmax(x, jnp.logical_and(emask, lane != i1))
    e2 = jnp.exp(v2 - v1)
    g1 = p_sel / (1.0 + e2)
    g2 = p_sel * e2 / (1.0 + e2)
    out = jnp.where(lane == 0, (i1 - n_groups).astype(F32), 0.0)
    out = jnp.where(lane == 1, (i2 - n_groups).astype(F32), out)
    out = jnp.where(lane == 2, g1, out)
    return jnp.where(lane == 3, g2, out)


def _merge_kernel(osb_ref, od_ref, h_ref, wo_ref, g_ref, b_ref, wr_ref, br_ref, *rest,
                  alpha, n_groups, per_group, aliased, n_tiles):
    h1_ref, r_ref = rest[2:] if aliased else rest
    sbw = osb_ref.shape[1]

    @pl.when(pl.program_id(0) < n_tiles)
    def _():
        att = (jnp.dot(osb_ref[...].astype(BF16), wo_ref[:sbw], preferred_element_type=F32)
               + jnp.dot(od_ref[...].astype(BF16), wo_ref[sbw:], preferred_element_type=F32))
        h1 = _layer_norm(alpha * h_ref[...] + att, g_ref[...], b_ref[...])
        h1_ref[...] = h1
        logits = jnp.dot(h1.astype(BF16), wr_ref[...], preferred_element_type=F32)
        r_ref[...] = _route(logits, br_ref[...], n_groups, per_group)

    @pl.when(pl.program_id(0) >= n_tiles)
    def _():
        h1_ref[...] = jnp.zeros_like(h1_ref)
        r_ref[...] = jnp.zeros_like(r_ref)


def _merge(osb, od, h, wo, g, b, wr, br, total_rows, tile_offset, prev, alpha, n_groups, per_group):
    n, d = h.shape
    tm = ROW_TILE
    assert n % tm == 0 and total_rows % tm == 0
    n_tiles = n // tm
    grid = n_tiles if prev is not None else total_rows // tm
    row = lambda wd: pl.BlockSpec((tm, wd), lambda i: (jnp.minimum(i, n_tiles - 1), 0))
    const = lambda a: pl.BlockSpec(a.shape, lambda i: (0,) * a.ndim, pipeline_mode=pl.Buffered(1))
    out_row = lambda wd: pl.BlockSpec((tm, wd), lambda i: (i + tile_offset, 0))
    args = [osb, od, h, wo, g, b, wr, br]
    in_specs = [row(osb.shape[1]), row(od.shape[1]), row(d), const(wo), const(g), const(b), const(wr),
                const(br)]
    aliases = {}
    if prev is not None:
        args += list(prev)
        in_specs += [pl.BlockSpec(memory_space=pl.ANY)] * 2
        aliases = {8: 0, 9: 1}
    return pl.pallas_call(
        functools.partial(_merge_kernel, alpha=alpha, n_groups=n_groups, per_group=per_group,
                          aliased=prev is not None, n_tiles=n_tiles),
        grid=(grid,),
        in_specs=in_specs,
        out_specs=(out_row(d), out_row(LANES)),
        out_shape=(jax.ShapeDtypeStruct((total_rows, d), F32),
                   jax.ShapeDtypeStruct((total_rows, LANES), F32)),
        input_output_aliases=aliases,
        compiler_params=_params(("arbitrary",)),
    )(*args)


def _moe_kernel(blk_exp_ref, n_used_ref, n_valid_ref, blk_base_ref, order_ref, x_hbm, wgu_ref, wd_ref,
                y_hbm, xbuf, ybuf, gsem, ssem):
    i = pl.program_id(0)
    tm, d = xbuf.shape[1:]
    d_exp = wd_ref.shape[1]
    n_used = n_used_ref[0]
    n_tok = y_hbm.shape[0] - tm
    slot = i % 2

    def gather_copies(blk, s):
        base = blk_base_ref[blk]
        return [pltpu.make_async_copy(x_hbm.at[pl.ds(order_ref[base + r] >> TOP_K_SHIFT, 1)],
                                      xbuf.at[s, pl.ds(r, 1)], gsem.at[s]) for r in range(tm)]

    def scatter_copies(blk, s, n_valid):
        base = blk_base_ref[blk]
        out = []
        for r in range(tm):
            a = order_ref[base + r]
            row = jnp.where(r < n_valid, a >> TOP_K_SHIFT, n_tok + r)
            col = pl.multiple_of(jnp.where(r < n_valid, a & (TOP_K - 1), s) * d, d)
            out.append(pltpu.make_async_copy(ybuf.at[s, pl.ds(r, 1)],
                                             y_hbm.at[pl.ds(row, 1), pl.ds(col, d)], ssem.at[s]))
        return out

    def wait_gather(s):
        pltpu.make_async_copy(x_hbm.at[pl.ds(0, tm)], xbuf.at[s], gsem.at[s]).wait()

    def wait_scatter(s):
        pltpu.make_async_copy(ybuf.at[s], y_hbm.at[pl.ds(0, tm), pl.ds(0, d)], ssem.at[s]).wait()

    @pl.when(i == 0)
    def _():
        xbuf[...] = jnp.zeros_like(xbuf)
        ybuf[...] = jnp.zeros_like(ybuf)
        for k in range(TOP_K):
            fill = pltpu.make_async_copy(ybuf.at[0], y_hbm.at[pl.ds(n_tok, tm), pl.ds(k * d, d)], ssem.at[0])
            fill.start()
            fill.wait()
        for cp in gather_copies(0, 0):
            cp.start()

    @pl.when(jnp.logical_and(i >= 1, i < n_used))
    def _():
        wait_scatter(slot)

    @pl.when(i < n_used)
    def _():
        wait_gather(slot)
        for cp in gather_copies(jnp.minimum(i + 1, n_used - 1), 1 - slot):
            cp.start()
        prev_valid = jnp.where(i >= 1, n_valid_ref[jnp.maximum(i - 1, 0)], 0)
        for cp in scatter_copies(jnp.maximum(i - 1, 0), 1 - slot, prev_valid):
            cp.start()
        x = xbuf[slot].astype(BF16)
        gu = jnp.dot(x, wgu_ref[0], preferred_element_type=F32)
        gate, up = gu[:, :d_exp], gu[:, d_exp:]
        hmid = (gate * jax.nn.sigmoid(gate) * up).astype(BF16)
        ybuf[slot] = jnp.dot(hmid, wd_ref[0], preferred_element_type=F32)

    @pl.when(i == n_used - 1)
    def _():
        wait_gather(1 - slot)
        wait_scatter(1 - slot)
        for cp in scatter_copies(i, slot, n_valid_ref[i]):
            cp.start()
        wait_scatter(slot)


def _moe(blk_exp, n_used, n_valid, blk_base, order, x_all, wgu16, wd16):
    n_blocks = blk_exp.shape[0]
    tm = ROW_TILE
    n_tok, d = x_all.shape
    grid_spec = pltpu.PrefetchScalarGridSpec(
        num_scalar_prefetch=5,
        grid=(n_blocks,),
        in_specs=[pl.BlockSpec(memory_space=pl.ANY),
                  pl.BlockSpec((1,) + wgu16.shape[1:], lambda i, be, *_: (be[i], 0, 0)),
                  pl.BlockSpec((1,) + wd16.shape[1:], lambda i, be, *_: (be[i], 0, 0))],
        out_specs=pl.BlockSpec(memory_space=pl.ANY),
        scratch_shapes=[pltpu.VMEM((2, tm, d), F32), pltpu.VMEM((2, tm, d), F32),
                        pltpu.SemaphoreType.DMA((2,)), pltpu.SemaphoreType.DMA((2,))],
    )
    return pl.pallas_call(
        _moe_kernel,
        grid_spec=grid_spec,
        out_shape=jax.ShapeDtypeStruct((n_tok + tm, TOP_K * d), F32),
        compiler_params=_params(("arbitrary",)),
    )(blk_exp, n_used, n_valid, blk_base, order, x_all, wgu16, wd16)


def _dispatch_tables(expert_idx, n_experts, tm):
    n_tok = expert_idx.shape[0]
    n_assign = n_tok * TOP_K
    flat_e = expert_idx.reshape(-1)
    order = jnp.argsort(flat_e).astype(I32)
    experts = jnp.arange(n_experts, dtype=I32)
    counts = jnp.sum(flat_e[:, None] == experts[None, :], axis=0, dtype=I32)
    start = jnp.cumsum(counts) - counts
    blocks_of = (counts + tm - 1) // tm
    blk_end = jnp.cumsum(blocks_of)
    n_blocks = -(-n_assign // tm) + n_experts
    n_used = blk_end[-1]
    blk = jnp.minimum(jnp.arange(n_blocks, dtype=I32), n_used - 1)
    blk_exp = jnp.sum(blk_end[None, :] <= blk[:, None], axis=1, dtype=I32)
    local = blk - (blk_end - blocks_of)[blk_exp]
    blk_base = start[blk_exp] + local * tm
    n_valid = jnp.clip(counts[blk_exp] - local * tm, 0, tm)
    order = jnp.concatenate([order, jnp.zeros((tm,), I32)])
    return blk_exp, n_used.reshape(1), n_valid, blk_base, order


def _combine_rows(y_ref, r_ref, h_ref, g_ref, b_ref, alpha):
    d = h_ref.shape[1]
    r = r_ref[...]
    ffn = y_ref[:, :d] * r[:, 2:3] + y_ref[:, d:] * r[:, 3:4]
    return _layer_norm(alpha * h_ref[...] + ffn, g_ref[...], b_ref[...])


def _combine_kernel(y_ref, r_ref, h_ref, g_ref, b_ref, o_ref, *, alpha):
    o_ref[...] = _combine_rows(y_ref, r_ref, h_ref, g_ref, b_ref, alpha)


def _combine(y_pairs, r_all, h_all, g, b, alpha, first_tile, n_tiles):
    d = h_all.shape[1]
    tm = ROW_TILE
    row = lambda wd: pl.BlockSpec((tm, wd), lambda i: (i + first_tile, 0))
    const = lambda a: pl.BlockSpec(a.shape, lambda i: (0,) * a.ndim)
    return pl.pallas_call(
        functools.partial(_combine_kernel, alpha=alpha),
        grid=(n_tiles,),
        in_specs=[row(2 * d), row(LANES), row(d), const(g), const(b)],
        out_specs=pl.BlockSpec((tm, d), lambda i: (i, 0)),
        out_shape=jax.ShapeDtypeStruct((n_tiles * tm, d), F32),
        compiler_params=_params(("arbitrary",)),
    )(y_pairs, r_all, h_all, g, b)


def _combine_prompt_kernel(y_ref, r_ref, h_ref, g_ref, b_ref, out_hbm, obuf, sem, *, alpha, n_meta,
                           tiles_per_seq):
    i = pl.program_id(0)
    n_steps = pl.num_programs(0)
    tm = obuf.shape[1]
    seq = out_hbm.shape[1]
    last_rows = seq + n_meta - (tiles_per_seq - 1) * tm
    kinds = ("first", "middle", "last")

    def kind_is(j, kind):
        return {"first": j == 0, "last": j == tiles_per_seq - 1,
                "middle": jnp.logical_and(j > 0, j < tiles_per_seq - 1)}[kind]

    def tile_copy(step, kind):
        b, j, s = step // tiles_per_seq, step % tiles_per_seq, step % 2
        if kind == "first":
            src, dst = obuf.at[s, pl.ds(n_meta, tm - n_meta)], out_hbm.at[b, pl.ds(0, tm - n_meta)]
        elif kind == "last":
            src = obuf.at[s, pl.ds(0, last_rows)]
            dst = out_hbm.at[b, pl.ds((tiles_per_seq - 1) * tm - n_meta, last_rows)]
        else:
            src, dst = obuf.at[s], out_hbm.at[b, pl.ds(pl.multiple_of(j * tm - n_meta, 8), tm)]
        return pltpu.make_async_copy(src, dst, sem.at[s])

    def for_kind(step, action):
        for kind in kinds:
            @pl.when(kind_is(step % tiles_per_seq, kind))
            def _():
                action(tile_copy(step, kind))

    @pl.when(i >= 2)
    def _():
        for_kind(i - 2, lambda cp: cp.wait())

    obuf[i % 2] = _combine_rows(y_ref, r_ref, h_ref, g_ref, b_ref, alpha)
    for_kind(i, lambda cp: cp.start())

    @pl.when(i == n_steps - 1)
    def _():
        @pl.when(i >= 1)
        def _():
            for_kind(i - 1, lambda cp: cp.wait())
        for_kind(i, lambda cp: cp.wait())


def _combine_prompt(y_pairs, r_all, h_all, g, b, alpha, bsz, lp, seq, n_meta):
    d = h_all.shape[1]
    tm = PROJ_TILE
    tiles_per_seq = lp // tm
    assert lp % tm == 0 and tiles_per_seq >= 2 and n_meta % 8 == 0 and n_meta < tm
    assert 0 < seq + n_meta - (tiles_per_seq - 1) * tm <= tm and (seq + n_meta) % 8 == 0
    row = lambda wd: pl.BlockSpec((tm, wd), lambda i: (i, 0))
    const = lambda a: pl.BlockSpec(a.shape, lambda i: (0,) * a.ndim)
    return pl.pallas_call(
        functools.partial(_combine_prompt_kernel, alpha=alpha, n_meta=n_meta, tiles_per_seq=tiles_per_seq),
        grid=(bsz * tiles_per_seq,),
        in_specs=[row(2 * d), row(LANES), row(d), const(g), const(b)],
        out_specs=pl.BlockSpec(memory_space=pl.ANY),
        out_shape=jax.ShapeDtypeStruct((bsz, seq, d), F32),
        scratch_shapes=[pltpu.VMEM((2, tm, d), F32), pltpu.SemaphoreType.DMA((2,))],
        compiler_params=_params(("arbitrary",)),
    )(y_pairs, r_all, h_all, g, b)


def kernel(x_prompt, x_sample, cache_k_sb, cache_v_sb, cache_k_diff, cache_v_diff, page_table, meta_tokens, ln_in_g, ln_in_b, rel_bias, w_in, sb_norm_g, diff_lambda, diff_subln_g, w_out, ln1_g, ln1_b, w_group, b_group, w_expert_router, b_expert_router, w_gate_up, w_down, ln2_g, ln2_b):
    bsz, seq, d = x_prompt.shape
    db = x_sample.shape[0]
    assert x_sample.shape[1] == 1
    depth = w_in.shape[0]
    assert depth == 1, "the prompt/decode buffers below are laid out for a single layer"
    n_meta = meta_tokens.shape[0]
    hd = sb_norm_g.shape[-1]
    sb_kv, diff_kv = cache_k_sb.shape[3], cache_k_diff.shape[3]
    sb_w, diff_w = d // 2, d // 2
    sb_kv_w, diff_kv_w = sb_kv * hd, diff_kv * 2 * hd
    splits = [0, sb_w, sb_w + sb_kv_w, sb_w + 2 * sb_kv_w, sb_w + 2 * sb_kv_w + diff_w,
              sb_w + 2 * sb_kv_w + diff_w + diff_kv_w, sb_w + 2 * sb_kv_w + diff_w + 2 * diff_kv_w]
    assert splits[-1] == w_in.shape[-1]
    n_groups = w_group.shape[-1]
    n_experts = w_expert_router.shape[-1]
    per_group = n_experts // n_groups
    page = cache_k_sb.shape[2]
    past = page_table.shape[1] * page
    alpha = (2 * depth) ** 0.25
    li = 0
    lam_init = 0.8 - 0.6 * math.exp(-0.3 * li)

    seq_len = n_meta + seq
    lp = -(-seq_len // (LANES * PROJ_TILE // math.gcd(LANES, PROJ_TILE))) * (LANES * PROJ_TILE // math.gcd(LANES, PROJ_TILE))
    xpad = jnp.concatenate([jnp.broadcast_to(meta_tokens[None], (bsz, n_meta, d)), x_prompt,
                            jnp.zeros((bsz, lp - seq_len, d), F32)], axis=1)
    g_in, b_in = ln_in_g.reshape(1, d), ln_in_b.reshape(1, d)
    w_in16 = w_in[li].astype(BF16)
    w_out16 = w_out[li].astype(BF16)
    wgu16 = w_gate_up[li].astype(BF16)
    wd16 = w_down[li].astype(BF16)
    w_router = jnp.zeros((d, LANES), F32).at[:, :n_groups].set(w_group[li]) \
        .at[:, n_groups:n_groups + n_experts].set(w_expert_router[li])
    b_router = jnp.zeros((1, LANES), F32).at[0, :n_groups].set(b_group[li]) \
        .at[0, n_groups:n_groups + n_experts].set(b_expert_router[li])
    w_router = w_router.astype(BF16)
    g1, b1 = ln1_g[li].reshape(1, d), ln1_b[li].reshape(1, d)
    g2, b2 = ln2_g[li].reshape(1, d), ln2_b[li].reshape(1, d)
    g_sb = sb_norm_g[li].reshape(1, hd)
    g_diff = diff_subln_g[li].reshape(1, 2 * hd)
    lam_params = diff_lambda[li]
    idx = jnp.arange(MXU_DIM)
    tri_half = (idx[:LANES, None] >= idx[None, :LANES])
    tri = jnp.concatenate([tri_half, jnp.ones((LANES, LANES), bool)], axis=1).astype(BF16)
    tri = jnp.concatenate([tri, tri], axis=0)

    near, far, dec = _bias_tables(rel_bias, past)

    (hp, qsb16, qd16, ksb16, vsb16, kd16, vd16, k_sb, v_sb, k_d, v_d) = _ln_proj_prompt(
        xpad, g_in, b_in, w_in16, splits, seq_len, sb_kv, diff_kv)
    o_sb = _sb_prompt(qsb16, ksb16, vsb16, tri, g_sb, sb_kv)
    o_d = _diff_prompt(qd16, kd16, vd16, near, far, lam_params, g_diff, diff_kv, lam_init)
    n_prompt = bsz * lp
    total_rows = n_prompt + ROW_TILE
    assert n_prompt % ROW_TILE == 0 and db <= ROW_TILE
    h1_all, r_all = _merge(o_sb.reshape(n_prompt, sb_w), o_d.reshape(n_prompt, diff_w),
                           hp.reshape(n_prompt, d), w_out16, g1, b1, w_router, b_router, total_rows, 0,
                           None, alpha, n_groups, per_group)

    hs, proj_s = _ln_proj_rows(x_sample.reshape(db, d), g_in, b_in, w_in16)
    qn_sb, kn_sb, vn_sb, qn_d, kn_d, vn_d = [proj_s[:, splits[t]:splits[t + 1]] for t in range(6)]
    pt_flat = page_table.reshape(-1).astype(I32)
    os_sb = _sb_decode(pt_flat, qn_sb, tri, g_sb, cache_k_sb, cache_v_sb, li)
    dec_heads = dec[:, 0, :].reshape(diff_kv, -1, past)
    dec_rows = jnp.concatenate([dec_heads] * 2, axis=1)
    bias0 = jnp.concatenate([(rel_bias[0] * LOG2E).reshape(diff_kv, -1, 1)] * 2, axis=1)
    os_d = _diff_decode(pt_flat, qn_d, kn_d, vn_d, dec_rows, bias0, lam_params, g_diff, cache_k_diff,
                        cache_v_diff, li, lam_init)
    pad_rows = lambda a: jnp.concatenate([a, jnp.zeros((ROW_TILE - db, a.shape[1]), a.dtype)], axis=0)
    h1_all, r_all = _merge(pad_rows(os_sb), pad_rows(os_d), pad_rows(hs), w_out16, g1, b1, w_router,
                           b_router, total_rows, n_prompt // ROW_TILE, (h1_all, r_all), alpha, n_groups,
                           per_group)

    expert_idx = r_all[:, :TOP_K].astype(I32)
    tables = _dispatch_tables(expert_idx, n_experts, ROW_TILE)
    y_pairs = _moe(*tables, h1_all, wgu16, wd16)
    y_prompt = _combine_prompt(y_pairs, r_all, h1_all, g2, b2, alpha, bsz, lp, seq, n_meta)
    y_sample = _combine(y_pairs, r_all, h1_all, g2, b2, alpha, n_prompt // ROW_TILE, 1)[:db].reshape(db, 1, d)
    return (y_prompt, y_sample,
            k_sb[None], v_sb[None], k_d[None], v_d[None],
            kn_sb.reshape(1, db, 1, sb_kv, hd), vn_sb.reshape(1, db, 1, sb_kv, hd),
            kn_d.reshape(1, db, 1, diff_kv, 2 * hd), vn_d.reshape(1, db, 1, diff_kv, 2 * hd))
```

```python
import functools
import math

import jax
import jax.numpy as jnp
from jax import lax
from jax.experimental import pallas as pl
from jax.experimental.pallas import tpu as pltpu

F32 = jnp.float32
BF16 = jnp.bfloat16
I32 = jnp.int32

LANES = 128
MXU_DIM = 256
LN_EPS = 1e-5
REL_BUCKETS, REL_MAX_EXACT, REL_MAX_DIST = 32, 16, 128
TOP_K = 2
TOP_K_SHIFT = TOP_K.bit_length() - 1
LOG2E = math.log2(math.e)
MASKED = -1e30
SB_UNDERFLOW = -104.0
ROW_TILE = 256
PROJ_TILE = 320
DEC_PAGES = 32
FAR_TILES = 8
VMEM_LIMIT = 56 * 1024 * 1024

NT_DIMS = (((1,), (1,)), ((), ()))


def _params(sem, vmem=VMEM_LIMIT):
    return pltpu.CompilerParams(dimension_semantics=sem, vmem_limit_bytes=vmem)


def _layer_norm(x, g, b):
    mu = jnp.mean(x, axis=-1, keepdims=True)
    xc = x - mu
    var = jnp.mean(xc * xc, axis=-1, keepdims=True)
    return xc * lax.rsqrt(var + LN_EPS) * g + b


def _rms_norm(x, g):
    return x * lax.rsqrt(jnp.mean(x * x, axis=-1, keepdims=True) + LN_EPS) * g


def _split_bf16(x):
    hi = x.astype(BF16)
    lo = (x - hi.astype(F32)).astype(BF16)
    return hi, lo


def _lambda_value(lp, lam_init):
    a = jnp.sum(lp[0:1] * lp[1:2], axis=-1, keepdims=True)
    b = jnp.sum(lp[2:3] * lp[3:4], axis=-1, keepdims=True)
    return jnp.exp(a) - jnp.exp(b) + lam_init


def _ln_proj_prompt_kernel(x_ref, g_ref, b_ref, w_ref, h_ref, qsb_ref, qd_ref,
                           ksb16_ref, vsb16_ref, kd16_ref, vd16_ref,
                           ksb_ref, vsb_ref, kd_ref, vd_ref, *, splits):
    h = _layer_norm(x_ref[0], g_ref[...], b_ref[...])
    h_ref[0] = h
    hb = h.astype(BF16)

    def proj(seg):
        lo, hi = splits[seg], splits[seg + 1]
        return jnp.dot(hb, w_ref[:, lo:hi], preferred_element_type=F32)

    def emit(seg, full_ref, half_ref):
        y = proj(seg)
        width = full_ref.shape[3]
        for h in range(full_ref.shape[2]):
            full_ref[0, :, h, :] = y[:, h * width:(h + 1) * width]
        half_ref[0] = y.astype(BF16)

    qsb_ref[0] = proj(0).astype(BF16)
    emit(1, ksb_ref, ksb16_ref)
    emit(2, vsb_ref, vsb16_ref)
    qd_ref[0] = proj(3).astype(BF16)
    emit(4, kd_ref, kd16_ref)
    emit(5, vd_ref, vd16_ref)


def _ln_proj_prompt(xpad, g, b, w16, splits, seq_len, sb_kv, diff_kv):
    bsz, lp, d = xpad.shape
    widths = [splits[i + 1] - splits[i] for i in range(6)]
    tm = PROJ_TILE
    assert lp % tm == 0 and lp - seq_len < tm
    row = lambda wd: pl.BlockSpec((1, tm, wd), lambda bi, i: (bi, i, 0))
    heads = lambda n, wd: pl.BlockSpec((1, tm, n, wd // n), lambda bi, i: (bi, i, 0, 0))
    const = lambda shp: pl.BlockSpec(shp, lambda bi, i: (0,) * len(shp))
    out_shape = (
        jax.ShapeDtypeStruct((bsz, lp, d), F32),
        jax.ShapeDtypeStruct((bsz, lp, widths[0]), BF16),
        jax.ShapeDtypeStruct((bsz, lp, widths[3]), BF16),
        jax.ShapeDtypeStruct((bsz, lp, widths[1]), BF16),
        jax.ShapeDtypeStruct((bsz, lp, widths[2]), BF16),
        jax.ShapeDtypeStruct((bsz, lp, widths[4]), BF16),
        jax.ShapeDtypeStruct((bsz, lp, widths[5]), BF16),
        jax.ShapeDtypeStruct((bsz, seq_len, sb_kv, widths[1] // sb_kv), F32),
        jax.ShapeDtypeStruct((bsz, seq_len, sb_kv, widths[2] // sb_kv), F32),
        jax.ShapeDtypeStruct((bsz, seq_len, diff_kv, widths[4] // diff_kv), F32),
        jax.ShapeDtypeStruct((bsz, seq_len, diff_kv, widths[5] // diff_kv), F32),
    )
    out_specs = (row(d), row(widths[0]), row(widths[3]), row(widths[1]), row(widths[2]),
                 row(widths[4]), row(widths[5]), heads(sb_kv, widths[1]), heads(sb_kv, widths[2]),
                 heads(diff_kv, widths[4]), heads(diff_kv, widths[5]))
    return pl.pallas_call(
        functools.partial(_ln_proj_prompt_kernel, splits=tuple(splits)),
        grid=(bsz, lp // tm),
        in_specs=[row(d), const((1, d)), const((1, d)),
                  pl.BlockSpec(w16.shape, lambda bi, i: (0, 0), pipeline_mode=pl.Buffered(1))],
        out_specs=out_specs,
        out_shape=out_shape,
        compiler_params=_params(("arbitrary", "arbitrary")),
    )(xpad, g, b, w16)


def _ln_proj_rows_kernel(x_ref, g_ref, b_ref, w_ref, h_ref, p_ref):
    h = _layer_norm(x_ref[...], g_ref[...], b_ref[...])
    h_ref[...] = h
    p_ref[...] = jnp.dot(h.astype(BF16), w_ref[...], preferred_element_type=F32)


def _ln_proj_rows(x, g, b, w):
    n, d = x.shape
    pw = w.shape[1]
    tn = 512
    assert pw % tn == 0
    return pl.pallas_call(
        _ln_proj_rows_kernel,
        grid=(pw // tn,),
        in_specs=[pl.BlockSpec((n, d), lambda j: (0, 0)), pl.BlockSpec((1, d), lambda j: (0, 0)),
                  pl.BlockSpec((1, d), lambda j: (0, 0)), pl.BlockSpec((d, tn), lambda j: (0, j))],
        out_specs=(pl.BlockSpec((n, d), lambda j: (0, 0)), pl.BlockSpec((n, tn), lambda j: (0, j))),
        out_shape=(jax.ShapeDtypeStruct((n, d), F32), jax.ShapeDtypeStruct((n, pw), F32)),
        compiler_params=_params(("arbitrary",)),
    )(x, g, b, w)


def _bias_from_dist(dist, rel_ref, head):
    n = jnp.maximum(dist, 0)
    nf = jnp.maximum(n, 1).astype(F32)
    large = REL_MAX_EXACT + (jnp.log(nf / REL_MAX_EXACT) / math.log(REL_MAX_DIST / REL_MAX_EXACT)
                             * (REL_BUCKETS - REL_MAX_EXACT)).astype(I32)
    large = jnp.minimum(large, REL_BUCKETS - 1)
    bucket = jnp.where(n < REL_MAX_EXACT, n, large)
    out = jnp.zeros(dist.shape, F32)
    for bkt in range(REL_BUCKETS):
        out = jnp.where(bucket == bkt, rel_ref[bkt, head], out)
    return out * LOG2E


def _bias_tables_kernel(rel_ref, near_ref, far_ref, dec_ref, *, n_heads, past):
    i = lax.broadcasted_iota(I32, (LANES, 2 * LANES), 0)
    j = lax.broadcasted_iota(I32, (LANES, 2 * LANES), 1)
    dist = i + LANES - j
    kpos = lax.broadcasted_iota(I32, dec_ref.shape[1:], 1)
    for h in range(n_heads):
        near_ref[h] = jnp.where(dist >= 0, _bias_from_dist(dist, rel_ref, h), MASKED)
        far_ref[h] = jnp.full(far_ref.shape[1:], rel_ref[REL_BUCKETS - 1, h] * LOG2E, F32)
        dec_ref[h] = _bias_from_dist(past - kpos, rel_ref, h)


def _bias_tables(rel_bias, past):
    n_heads = rel_bias.shape[1]
    return pl.pallas_call(
        functools.partial(_bias_tables_kernel, n_heads=n_heads, past=past),
        in_specs=[pl.BlockSpec(memory_space=pltpu.SMEM)],
        out_shape=(jax.ShapeDtypeStruct((n_heads, LANES, 2 * LANES), F32),
                   jax.ShapeDtypeStruct((n_heads, LANES, LANES), F32),
                   jax.ShapeDtypeStruct((n_heads, 8, past), F32)),
    )(rel_bias)


def _sb_tile(z, vis, tri, carry):
    l = -(jnp.maximum(z, 0.0) + jnp.log(1.0 + jnp.exp(-jnp.abs(z))))
    if vis is not None:
        l = jnp.where(vis, l, 0.0)
    hi, lo = _split_bf16(l)
    t2 = jnp.dot(jnp.concatenate([hi, lo], axis=1), tri, preferred_element_type=F32)
    nt = z.shape[1]
    incl, total = t2[:, :nt], t2[:, nt:]
    w = jnp.exp(z + incl + carry)
    if vis is not None:
        w = jnp.where(vis, w, 0.0)
    return w, carry + total


def _sb_prompt_kernel(q_ref, k_ref, v_ref, tri_ref, g_ref, o_ref, acc_ref, carry_ref, *, scale):
    qi = pl.program_id(0)
    bsz, n_kv, m, hd = acc_ref.shape
    group = m // LANES
    qh = [[jnp.concatenate([q_ref[b, :, (h * group + g) * hd:(h * group + g + 1) * hd]
                            for g in range(group)], axis=0) for h in range(n_kv)] for b in range(bsz)]
    acc_ref[...] = jnp.zeros_like(acc_ref)
    carry_ref[...] = jnp.zeros_like(carry_ref)
    chains = [(b, h) for b in range(bsz) for h in range(n_kv)]
    n_rows = len(chains) * m
    rowpos = lax.broadcasted_iota(I32, (n_rows, LANES), 0) & (LANES - 1)
    col = lax.broadcasted_iota(I32, (n_rows, LANES), 1)
    tri = tri_ref[...]

    def cond(state):
        j, live = state
        return jnp.logical_and(j >= 0, live > SB_UNDERFLOW)

    def body(state):
        j, _ = state
        ks = pl.multiple_of(j * LANES, LANES)
        z = jnp.concatenate(
            [lax.dot_general(qh[b][h], k_ref[b, pl.ds(ks, LANES), h * hd:(h + 1) * hd], NT_DIMS,
                             preferred_element_type=F32) for b, h in chains], axis=0) * scale
        vis = col < rowpos + (qi - j) * LANES
        w, carry = _sb_tile(z, vis, tri, carry_ref[...])
        carry_ref[...] = carry
        w = w.astype(BF16)
        for n, (b, h) in enumerate(chains):
            acc_ref[b, h] += jnp.dot(w[n * m:(n + 1) * m], v_ref[b, pl.ds(ks, LANES), h * hd:(h + 1) * hd],
                                     preferred_element_type=F32)
        return j - 1, jnp.max(carry)

    lax.while_loop(cond, body, (qi, jnp.float32(0.0)))
    for b in range(bsz):
        outs = []
        for h in range(n_kv):
            o = _rms_norm(acc_ref[b, h], g_ref[...]).astype(o_ref.dtype)
            outs += [o[g * LANES:(g + 1) * LANES] for g in range(group)]
        o_ref[b] = jnp.concatenate(outs, axis=1)


def _sb_prompt(q16, k16, v16, tri, g, n_kv):
    bsz, lp, width = q16.shape
    kvw = k16.shape[2]
    hd = kvw // n_kv
    m = width // n_kv // hd * LANES
    resident = lambda: pl.BlockSpec((bsz, lp, kvw), lambda i: (0, 0, 0), pipeline_mode=pl.Buffered(1))
    return pl.pallas_call(
        functools.partial(_sb_prompt_kernel, scale=hd ** -0.5),
        grid=(lp // LANES,),
        in_specs=[pl.BlockSpec((bsz, LANES, width), lambda i: (0, i, 0)),
                  resident(), resident(),
                  pl.BlockSpec(tri.shape, lambda i: (0, 0)),
                  pl.BlockSpec((1, hd), lambda i: (0, 0))],
        out_specs=pl.BlockSpec((bsz, LANES, width), lambda i: (0, i, 0)),
        out_shape=jax.ShapeDtypeStruct((bsz, lp, width), BF16),
        scratch_shapes=[pltpu.VMEM((bsz, n_kv, m, hd), F32), pltpu.VMEM((bsz * n_kv * m, LANES), F32)],
        compiler_params=_params(("arbitrary",)),
    )(q16, k16, v16, tri, g)


def _sb_decode_kernel(pt_ref, q_ref, tri_ref, g_ref, k_hbm, v_hbm, o_ref, kbuf, vbuf, sem, *,
                      scale, n_kv, n_pages, layer):
    s = pl.program_id(0)
    q = q_ref[0]
    hd = g_ref.shape[1]
    n_heads = q.shape[1] // hd
    group = n_heads // n_kv
    zeros = jnp.zeros((1, hd), F32)
    rows = []
    for n in range(n_heads):
        seg = q[:, n * hd:(n + 1) * hd]
        rows.append(jnp.concatenate([seg if h == n // group else zeros for h in range(n_kv)], axis=1))
    qb = jnp.concatenate(rows, axis=0)
    q16 = qb.astype(BF16)
    tri = tri_ref[...]

    def page_copies(j, slot):
        page = pt_ref[s * n_pages + j]
        return [pltpu.make_async_copy(hbm.at[layer, page, :, h, :], buf.at[slot, h], sem.at[which, slot])
                for which, (hbm, buf) in enumerate(((k_hbm, kbuf), (v_hbm, vbuf))) for h in range(n_kv)]

    def fetch(j, slot):
        for cp in page_copies(j, slot):
            cp.start()

    def wait_page(j, slot):
        for cp in page_copies(j, slot):
            cp.wait()

    def heads_on_lanes(buf, slot):
        return jnp.concatenate([buf[slot, h] for h in range(n_kv)], axis=1)

    fetch(n_pages - 1, 0)

    def cond(state):
        j, live, _, _ = state
        return jnp.logical_and(j >= 0, live > SB_UNDERFLOW)

    def body(state):
        j, _, carry, acc = state
        slot = (n_pages - 1 - j) % 2
        wait_page(j, slot)

        @pl.when(j > 0)
        def _():
            fetch(j - 1, 1 - slot)

        k = heads_on_lanes(kbuf, slot).astype(BF16)
        v = heads_on_lanes(vbuf, slot).astype(BF16)
        z = lax.dot_general(q16, k, NT_DIMS, preferred_element_type=F32) * scale
        w, carry = _sb_tile(z, None, tri, carry)
        acc = acc + jnp.dot(w.astype(BF16), v, preferred_element_type=F32)
        return j - 1, jnp.max(carry), carry, acc

    init = (jnp.int32(n_pages - 1), jnp.float32(0.0), jnp.zeros((n_heads, LANES), F32),
            jnp.zeros((n_heads, n_kv * hd), F32))
    j_end, _, _, acc = lax.while_loop(cond, body, init)

    @pl.when(j_end >= 0)
    def _():
        wait_page(j_end, (n_pages - 1 - j_end) % 2)

    outs = []
    for n in range(n_heads):
        h = n // group
        outs.append(_rms_norm(acc[n:n + 1, h * hd:(h + 1) * hd], g_ref[...]))
    o_ref[0] = jnp.concatenate(outs, axis=1)


def _sb_decode(pt_flat, q, tri, g, cache_k, cache_v, layer):
    db, width = q.shape
    depth, n_pool, ps, n_kv, hd = cache_k.shape
    assert ps == LANES
    n_pages = pt_flat.shape[0] // db
    grid_spec = pltpu.PrefetchScalarGridSpec(
        num_scalar_prefetch=1,
        grid=(db,),
        in_specs=[pl.BlockSpec((1, 1, width), lambda s, pt: (s, 0, 0)),
                  pl.BlockSpec(tri.shape, lambda s, pt: (0, 0)),
                  pl.BlockSpec((1, hd), lambda s, pt: (0, 0)),
                  pl.BlockSpec(memory_space=pl.ANY), pl.BlockSpec(memory_space=pl.ANY)],
        out_specs=pl.BlockSpec((1, 1, width), lambda s, pt: (s, 0, 0)),
        scratch_shapes=[pltpu.VMEM((2, n_kv, ps, hd), F32), pltpu.VMEM((2, n_kv, ps, hd), F32),
                        pltpu.SemaphoreType.DMA((2, 2))],
    )
    out = pl.pallas_call(
        functools.partial(_sb_decode_kernel, scale=hd ** -0.5, n_kv=n_kv, n_pages=n_pages, layer=layer),
        grid_spec=grid_spec,
        out_shape=jax.ShapeDtypeStruct((db, 1, width), F32),
        compiler_params=_params(("arbitrary",)),
    )(pt_flat, q.reshape(db, 1, width), tri, g, cache_k, cache_v)
    return out.reshape(db, width)


def _diff_prompt_kernel(q_ref, k_ref, v_ref, near_ref, far_ref, lam_ref, g_ref, o_ref,
                        m_ref, l_ref, acc_ref, *, scale, lam_init, far_tiles):
    qi = pl.program_id(1)
    n_kv = m_ref.shape[0]
    kw = k_ref.shape[2] // n_kv
    hd = kw // 2
    group = q_ref.shape[2] // (n_kv * kw)
    rows = group * LANES
    qm = [[jnp.concatenate([q_ref[0, :, ((h * group + g) * 2 + c) * hd:((h * group + g) * 2 + c + 1) * hd]
                            for g in range(group)], axis=0) for c in range(2)] for h in range(n_kv)]
    c1 = scale * LOG2E

    def scores(h, c, ks, width):
        k = k_ref[0, pl.ds(ks, width), h * kw + c * hd:h * kw + (c + 1) * hd]
        return lax.dot_general(qm[h][c], k, NT_DIMS, preferred_element_type=F32) * c1

    def lane_tiles(x):
        return [x[:, t * LANES:(t + 1) * LANES] for t in range(x.shape[1] // LANES)]

    def row_max(s):
        return jnp.max(functools.reduce(jnp.maximum, lane_tiles(s)), axis=-1, keepdims=True)

    def exp_tiles(s, shift):
        ps = [jnp.exp2(t - shift) for t in lane_tiles(s)]
        return jnp.concatenate(ps, axis=1).astype(BF16), functools.reduce(jnp.add, ps)

    head_maps = [(h, c) for h in range(n_kv) for c in range(2)]

    def near_chunk(ks, first_col, width):
        ss = [scores(h, c, ks, width) + near_ref[h, :, first_col:first_col + width] for h, c in head_maps]
        ps = []
        for (h, c), s in zip(head_maps, ss):
            mx = jnp.broadcast_to(row_max(s), (rows, LANES))
            p, psum = exp_tiles(s, mx)
            m_ref[h, 0, c] = mx
            l_ref[h, 0, c] = psum
            ps.append(p)
        for (h, c), p in zip(head_maps, ps):
            acc_ref[h, 0, c] = jnp.dot(p, v_ref[0, pl.ds(ks, width), h * kw:(h + 1) * kw],
                                       preferred_element_type=F32)

    @pl.when(qi == 0)
    def _():
        near_chunk(0, LANES, LANES)

    @pl.when(qi > 0)
    def _():
        near_chunk(pl.multiple_of((qi - 1) * LANES, LANES), 0, 2 * LANES)

    for h in range(n_kv):
        m_ref[h, 1] = jnp.full(m_ref.shape[2:], MASKED, F32)
        l_ref[h, 1] = jnp.zeros(l_ref.shape[2:], F32)
        acc_ref[h, 1] = jnp.zeros(acc_ref.shape[2:], F32)

    def far_chunks(chunks):
        for ks, stream, keep in chunks:
            for h, c in head_maps:
                s = scores(h, c, ks, wide)
                if keep is not None:
                    kpos = ks + lax.broadcasted_iota(I32, s.shape, 1)
                    s = jnp.where(jnp.logical_and(kpos >= keep[0], kpos < keep[1]), s, MASKED)
                fb = far_ref[h]
                m_old = m_ref[h, stream, c]
                m_new = jnp.maximum(m_old, row_max(s) + fb)
                p, psum = exp_tiles(s, m_new - fb)
                pv = jnp.dot(p, v_ref[0, pl.ds(ks, wide), h * kw:(h + 1) * kw], preferred_element_type=F32)
                alpha = jnp.exp2(m_old - m_new)
                l_ref[h, stream, c] = alpha * l_ref[h, stream, c] + psum
                acc_ref[h, stream, c] = (jnp.concatenate([alpha] * (kw // LANES), axis=1) * acc_ref[h, stream, c]
                                         + pv)
                m_ref[h, stream, c] = m_new

    n_far = jnp.maximum(qi - 1, 0)
    wide = far_tiles * LANES
    n_full = n_far // far_tiles

    def pair_body(t, carry):
        far_chunks([(pl.multiple_of(2 * t * wide, wide), 0, None)])
        far_chunks([(pl.multiple_of((2 * t + 1) * wide, wide), 1, None)])
        return carry

    lax.fori_loop(0, n_full // 2, pair_body, 0)

    @pl.when(n_full % 2 == 1)
    def _():
        far_chunks([(pl.multiple_of((n_full - 1) * wide, wide), 0, None)])

    @pl.when(n_far % far_tiles != 0)
    def _():
        ks = pl.multiple_of(jnp.minimum(n_full * wide, k_ref.shape[1] - wide), LANES)
        far_chunks([(ks, 1, (n_full * wide, n_far * LANES))])

    lam = _lambda_value(lam_ref[...], lam_init)
    outs = []
    for h in range(n_kv):
        merged = []
        for c in range(2):
            m = jnp.maximum(m_ref[h, 0, c], m_ref[h, 1, c])
            a0, a1 = jnp.exp2(m_ref[h, 0, c] - m), jnp.exp2(m_ref[h, 1, c] - m)
            l = jnp.sum(a0 * l_ref[h, 0, c] + a1 * l_ref[h, 1, c], axis=-1, keepdims=True)
            merged.append((a0[:, :1] * acc_ref[h, 0, c] + a1[:, :1] * acc_ref[h, 1, c]) / l)
        a = merged[0] - lam * merged[1]
        o = _rms_norm(a, g_ref[...] * (1.0 - lam_init)).astype(o_ref.dtype)
        outs += [o[g * LANES:(g + 1) * LANES] for g in range(group)]
    o_ref[0] = jnp.concatenate(outs, axis=1)


def _diff_prompt(q16, k16, v16, near, far, lam_params, g, n_kv, lam_init):
    bsz, lp, width = q16.shape
    kvw = k16.shape[2]
    kw = kvw // n_kv
    group = width // kvw
    rows = group * LANES
    nq = lp // LANES
    far_tiles = min(FAR_TILES, nq)
    near_g = near.reshape(n_kv, rows, 2 * LANES)
    far_g = far.reshape(n_kv, rows, LANES)
    resident = lambda: pl.BlockSpec((1, lp, kvw), lambda b, i: (b, 0, 0), pipeline_mode=pl.Buffered(1))
    const = lambda a: pl.BlockSpec(a.shape, lambda b, i: (0,) * a.ndim)
    return pl.pallas_call(
        functools.partial(_diff_prompt_kernel, scale=(kw // 2) ** -0.5, lam_init=lam_init,
                          far_tiles=far_tiles),
        grid=(bsz, nq),
        in_specs=[pl.BlockSpec((1, LANES, width), lambda b, i: (b, i, 0)), resident(), resident(),
                  const(near_g), const(far_g), const(lam_params), const(g)],
        out_specs=pl.BlockSpec((1, LANES, width), lambda b, i: (b, i, 0)),
        out_shape=jax.ShapeDtypeStruct((bsz, lp, width), BF16),
        scratch_shapes=[pltpu.VMEM((n_kv, 2, 2, rows, LANES), F32), pltpu.VMEM((n_kv, 2, 2, rows, LANES), F32),
                        pltpu.VMEM((n_kv, 2, 2, rows, kw), F32)],
        compiler_params=_params(("arbitrary",) * 2),
    )(q16, k16, v16, near_g, far_g, lam_params, g)


def _diff_decode_kernel(pt_ref, q_ref, kn_ref, vn_ref, bias_ref, bias0_ref, lam_ref, g_ref, k_hbm, v_hbm,
                        o_ref, kbuf, vbuf, sem, s_ref, snew_ref, m_ref, a_ref, anew_ref, acc_ref, *,
                        scale, lam_init, n_pages, pages, layer, n_seq):
    s = pl.program_id(0)
    c = pl.program_id(1)
    nc = n_pages // pages
    n_kv, span, kw = kbuf.shape[1:]
    ps = span // pages
    hd = kw // 2
    group = q_ref.shape[2] // (n_kv * kw)
    step = s * nc + c
    slot = step % 2
    c1 = scale * LOG2E

    def chunk_copies(step_idx, dst_slot):
        seq, ch = step_idx // nc, step_idx % nc
        base_k = jnp.minimum(seq, n_seq - 1) * n_pages + ch * pages
        base_v = jnp.maximum(seq - 1, 0) * n_pages + ch * pages
        out = []
        for t in range(pages):
            for which, (hbm, buf, base) in enumerate(((k_hbm, kbuf, base_k), (v_hbm, vbuf, base_v))):
                page = pt_ref[base + t]
                out += [pltpu.make_async_copy(hbm.at[layer, page, :, h, :],
                                              buf.at[dst_slot, h, pl.ds(t * ps, ps)], sem.at[which, dst_slot])
                        for h in range(n_kv)]
        return out

    @pl.when(step == 0)
    def _():
        for cp in chunk_copies(step, slot):
            cp.start()

    @pl.when(step + 1 < (n_seq + 1) * nc)
    def _():
        for cp in chunk_copies(step + 1, 1 - slot):
            cp.start()

    for cp in chunk_copies(step, slot):
        cp.wait()

    lam = _lambda_value(lam_ref[...], lam_init)
    cols = pl.ds(pl.multiple_of(c * span, span), span)

    @pl.when(s < n_seq)
    def _():
        q = q_ref[0]
        zeros = jnp.zeros((1, hd), F32)
        for h in range(n_kv):
            rows = []
            for mp in range(2):
                for g in range(group):
                    lo = ((h * group + g) * 2 + mp) * hd
                    rows.append(jnp.concatenate([q[:, lo:lo + hd], zeros] if mp == 0 else
                                                [zeros, q[:, lo:lo + hd]], axis=1))
            qb = jnp.concatenate(rows, axis=0).astype(BF16)

            @pl.when(c == 0)
            def _():
                kn = kn_ref[0][:, h * kw:(h + 1) * kw].astype(BF16)
                s0 = jnp.sum(qb.astype(F32) * kn.astype(F32), axis=-1, keepdims=True) * c1 + bias0_ref[h]
                snew_ref[h] = jnp.broadcast_to(s0, snew_ref.shape[1:])
                m_ref[h] = jnp.broadcast_to(s0, m_ref.shape[1:])

            sc = lax.dot_general(qb, kbuf[slot, h].astype(BF16), NT_DIMS,
                                 preferred_element_type=F32) * c1 + bias_ref[h]
            s_ref[h, :, cols] = sc
            m_ref[h] = jnp.maximum(m_ref[h], jnp.max(sc, axis=-1, keepdims=True))

            @pl.when(c == nc - 1)
            def _():
                m = m_ref[h][:, :1]
                p = jnp.exp2(s_ref[h] - m)
                p0 = jnp.exp2(snew_ref[h][:, :1] - m)
                l = jnp.sum(p, axis=-1, keepdims=True) + p0
                pn, pn0 = p / l, p0 / l
                a_ref[s % 2, h] = (pn[:group] - lam * pn[group:]).astype(a_ref.dtype)
                anew_ref[s % 2, h] = jnp.broadcast_to(pn0[:group] - lam * pn0[group:], anew_ref.shape[2:])

    @pl.when(s >= 1)
    def _():
        par = (s - 1) % 2
        for h in range(n_kv):
            @pl.when(c == 0)
            def _():
                a0 = anew_ref[par, h][:, :1].astype(BF16).astype(F32)
                vn = vn_ref[0][:, h * kw:(h + 1) * kw].astype(BF16).astype(F32)
                acc_ref[h] = a0 * vn

            acc_ref[h] += jnp.dot(a_ref[par, h, :, cols], vbuf[slot, h].astype(BF16),
                                  preferred_element_type=F32)

            @pl.when(c == nc - 1)
            def _():
                gain = g_ref[...] * (1.0 - lam_init)
                for g in range(group):
                    lo = (h * group + g) * kw
                    o_ref[0, :, lo:lo + kw] = _rms_norm(acc_ref[h][g:g + 1], gain)


def _diff_decode(pt_flat, q, k_new, v_new, dec_bias, bias0, lam_params, g, cache_k, cache_v, layer,
                 lam_init):
    db, width = q.shape
    depth, n_pool, ps, n_kv, kw = cache_k.shape
    n_pages = pt_flat.shape[0] // db
    pages = min(DEC_PAGES, n_pages)
    assert n_pages % pages == 0
    group = width // (n_kv * kw)
    n_rows = 2 * group
    span = pages * ps
    past = n_pages * ps

    cur = lambda wd: pl.BlockSpec((1, 1, wd), lambda s, c, pt: (jnp.minimum(s, db - 1), 0, 0))
    prev = lambda wd: pl.BlockSpec((1, 1, wd), lambda s, c, pt: (jnp.maximum(s - 1, 0), 0, 0))
    const = lambda shp: pl.BlockSpec(shp, lambda s, c, pt: (0,) * len(shp))
    grid_spec = pltpu.PrefetchScalarGridSpec(
        num_scalar_prefetch=1,
        grid=(db + 1, n_pages // pages),
        in_specs=[cur(width), cur(n_kv * kw), prev(n_kv * kw),
                  pl.BlockSpec((n_kv, n_rows, span), lambda s, c, pt: (0, 0, c)),
                  const((n_kv, n_rows, 1)), const(lam_params.shape), const((1, kw)),
                  pl.BlockSpec(memory_space=pl.ANY), pl.BlockSpec(memory_space=pl.ANY)],
        out_specs=prev(width),
        scratch_shapes=[pltpu.VMEM((2, n_kv, span, kw), F32),
                        pltpu.VMEM((2, n_kv, span, kw), F32),
                        pltpu.SemaphoreType.DMA((2, 2)),
                        pltpu.VMEM((n_kv, n_rows, past), F32),
                        pltpu.VMEM((n_kv, n_rows, LANES), F32),
                        pltpu.VMEM((n_kv, n_rows, LANES), F32),
                        pltpu.VMEM((2, n_kv, group, past), BF16),
                        pltpu.VMEM((2, n_kv, group, LANES), F32),
                        pltpu.VMEM((n_kv, group, kw), F32)],
    )
    out = pl.pallas_call(
        functools.partial(_diff_decode_kernel, scale=(kw // 2) ** -0.5, lam_init=lam_init,
                          n_pages=n_pages, pages=pages, layer=layer, n_seq=db),
        grid_spec=grid_spec,
        out_shape=jax.ShapeDtypeStruct((db, 1, width), F32),
        compiler_params=_params(("arbitrary",) * 2),
    )(pt_flat, q.reshape(db, 1, width), k_new.reshape(db, 1, n_kv * kw), v_new.reshape(db, 1, n_kv * kw),
      dec_bias, bias0, lam_params, g, cache_k, cache_v)
    return out.reshape(db, width)


def _route(logits, bias, n_groups, per_group):
    x = logits + bias
    lane = lax.broadcasted_iota(I32, x.shape, 1).astype(F32)
    big = jnp.float32(4 * LANES)
    neg = jnp.float32(-jnp.inf)

    def first_argmax(vals, mask):
        mx = jnp.max(jnp.where(mask, vals, neg), axis=-1, keepdims=True)
        idx = jnp.min(jnp.where(jnp.logical_and(mask, vals == mx), lane, big), axis=-1, keepdims=True)
        return mx, idx

    gmask = lane < n_groups
    gmax, grp = first_argmax(x, gmask)
    p_sel = 1.0 / jnp.sum(jnp.where(gmask, jnp.exp(x - gmax), 0.0), axis=-1, keepdims=True)
    emask = jnp.logical_and(lane >= n_groups + grp * per_group, lane < n_groups + (grp + 1) * per_group)
    v1, i1 = first_argmax(x, emask)
    v2, i2 = first_argmax(x, jnp.logical_and(emask, lane != i1))
    e2 = jnp.exp(v2 - v1)
    g1 = p_sel / (1.0 + e2)
    g2 = p_sel * e2 / (1.0 + e2)
    out = jnp.where(lane == 0, (i1 - n_groups).astype(F32), 0.0)
    out = jnp.where(lane == 1, (i2 - n_groups).astype(F32), out)
    out = jnp.where(lane == 2, g1, out)
    return jnp.where(lane == 3, g2, out)


def _merge_kernel(osb_ref, od_ref, h_ref, wo_ref, g_ref, b_ref, wr_ref, br_ref, *rest,
                  alpha, n_groups, per_group, aliased, n_tiles):
    h1_ref, r_ref = rest[2:] if aliased else rest
    sbw = osb_ref.shape[1]

    @pl.when(pl.program_id(0) < n_tiles)
    def _():
        att = (jnp.dot(osb_ref[...].astype(BF16), wo_ref[:sbw], preferred_element_type=F32)
               + jnp.dot(od_ref[...].astype(BF16), wo_ref[sbw:], preferred_element_type=F32))
        h1 = _layer_norm(alpha * h_ref[...] + att, g_ref[...], b_ref[...])
        h1_ref[...] = h1
        logits = jnp.dot(h1.astype(BF16), wr_ref[...], preferred_element_type=F32)
        r_ref[...] = _route(logits, br_ref[...], n_groups, per_group)

    @pl.when(pl.program_id(0) >= n_tiles)
    def _():
        h1_ref[...] = jnp.zeros_like(h1_ref)
        r_ref[...] = jnp.zeros_like(r_ref)


def _merge(osb, od, h, wo, g, b, wr, br, total_rows, tile_offset, prev, alpha, n_groups, per_group):
    n, d = h.shape
    tm = ROW_TILE
    assert n % tm == 0 and total_rows % tm == 0
    n_tiles = n // tm
    grid = n_tiles if prev is not None else total_rows // tm
    row = lambda wd: pl.BlockSpec((tm, wd), lambda i: (jnp.minimum(i, n_tiles - 1), 0))
    const = lambda a: pl.BlockSpec(a.shape, lambda i: (0,) * a.ndim, pipeline_mode=pl.Buffered(1))
    out_row = lambda wd: pl.BlockSpec((tm, wd), lambda i: (i + tile_offset, 0))
    args = [osb, od, h, wo, g, b, wr, br]
    in_specs = [row(osb.shape[1]), row(od.shape[1]), row(d), const(wo), const(g), const(b), const(wr),
                const(br)]
    aliases = {}
    if prev is not None:
        args += list(prev)
        in_specs += [pl.BlockSpec(memory_space=pl.ANY)] * 2
        aliases = {8: 0, 9: 1}
    return pl.pallas_call(
        functools.partial(_merge_kernel, alpha=alpha, n_groups=n_groups, per_group=per_group,
                          aliased=prev is not None, n_tiles=n_tiles),
        grid=(grid,),
        in_specs=in_specs,
        out_specs=(out_row(d), out_row(LANES)),
        out_shape=(jax.ShapeDtypeStruct((total_rows, d), F32),
                   jax.ShapeDtypeStruct((total_rows, LANES), F32)),
        input_output_aliases=aliases,
        compiler_params=_params(("arbitrary",)),
    )(*args)


def _moe_kernel(blk_exp_ref, n_used_ref, n_valid_ref, blk_base_ref, order_ref, x_hbm, wgu_ref, wd_ref,
                y_hbm, xbuf, ybuf, gsem, ssem):
    i = pl.program_id(0)
    tm, d = xbuf.shape[1:]
    d_exp = wd_ref.shape[1]
    n_used = n_used_ref[0]
    n_tok = y_hbm.shape[0] - tm
    slot = i % 2

    def gather_copies(blk, s):
        base = blk_base_ref[blk]
        return [pltpu.make_async_copy(x_hbm.at[pl.ds(order_ref[base + r] >> TOP_K_SHIFT, 1)],
                                      xbuf.at[s, pl.ds(r, 1)], gsem.at[s]) for r in range(tm)]

    def scatter_copies(blk, s, n_valid):
        base = blk_base_ref[blk]
        out = []
        for r in range(tm):
            a = order_ref[base + r]
            row = jnp.where(r < n_valid, a >> TOP_K_SHIFT, n_tok + r)
            col = pl.multiple_of(jnp.where(r < n_valid, a & (TOP_K - 1), s) * d, d)
            out.append(pltpu.make_async_copy(ybuf.at[s, pl.ds(r, 1)],
                                             y_hbm.at[pl.ds(row, 1), pl.ds(col, d)], ssem.at[s]))
        return out

    def wait_gather(s):
        pltpu.make_async_copy(x_hbm.at[pl.ds(0, tm)], xbuf.at[s], gsem.at[s]).wait()

    def wait_scatter(s):
        pltpu.make_async_copy(ybuf.at[s], y_hbm.at[pl.ds(0, tm), pl.ds(0, d)], ssem.at[s]).wait()

    @pl.when(i == 0)
    def _():
        xbuf[...] = jnp.zeros_like(xbuf)
        ybuf[...] = jnp.zeros_like(ybuf)
        for k in range(TOP_K):
            fill = pltpu.make_async_copy(ybuf.at[0], y_hbm.at[pl.ds(n_tok, tm), pl.ds(k * d, d)], ssem.at[0])
            fill.start()
            fill.wait()
        for cp in gather_copies(0, 0):
            cp.start()

    @pl.when(jnp.logical_and(i >= 1, i < n_used))
    def _():
        wait_scatter(slot)

    @pl.when(i < n_used)
    def _():
        wait_gather(slot)
        for cp in gather_copies(jnp.minimum(i + 1, n_used - 1), 1 - slot):
            cp.start()
        prev_valid = jnp.where(i >= 1, n_valid_ref[jnp.maximum(i - 1, 0)], 0)
        for cp in scatter_copies(jnp.maximum(i - 1, 0), 1 - slot, prev_valid):
            cp.start()
        x = xbuf[slot].astype(BF16)
        gu = jnp.dot(x, wgu_ref[0], preferred_element_type=F32)
        gate, up = gu[:, :d_exp], gu[:, d_exp:]
        hmid = (gate * jax.nn.sigmoid(gate) * up).astype(BF16)
        ybuf[slot] = jnp.dot(hmid, wd_ref[0].astype(BF16), preferred_element_type=F32)

    @pl.when(i == n_used - 1)
    def _():
        wait_gather(1 - slot)
        wait_scatter(1 - slot)
        for cp in scatter_copies(i, slot, n_valid_ref[i]):
            cp.start()
        wait_scatter(slot)


def _moe(blk_exp, n_used, n_valid, blk_base, order, x_all, wgu16, w_down):
    n_blocks = blk_exp.shape[0]
    tm = ROW_TILE
    n_tok, d = x_all.shape
    grid_spec = pltpu.PrefetchScalarGridSpec(
        num_scalar_prefetch=5,
        grid=(n_blocks,),
        in_specs=[pl.BlockSpec(memory_space=pl.ANY),
                  pl.BlockSpec((1,) + wgu16.shape[1:], lambda i, be, *_: (be[i], 0, 0)),
                  pl.BlockSpec((1,) + w_down.shape[1:], lambda i, be, *_: (be[i], 0, 0))],
        out_specs=pl.BlockSpec(memory_space=pl.ANY),
        scratch_shapes=[pltpu.VMEM((2, tm, d), F32), pltpu.VMEM((2, tm, d), F32),
                        pltpu.SemaphoreType.DMA((2,)), pltpu.SemaphoreType.DMA((2,))],
    )
    return pl.pallas_call(
        _moe_kernel,
        grid_spec=grid_spec,
        out_shape=jax.ShapeDtypeStruct((n_tok + tm, TOP_K * d), F32),
        compiler_params=_params(("arbitrary",)),
    )(blk_exp, n_used, n_valid, blk_base, order, x_all, wgu16, w_down)


def _dispatch_tables(expert_idx, n_experts, tm):
    n_tok = expert_idx.shape[0]
    n_assign = n_tok * TOP_K
    flat_e = expert_idx.reshape(-1)
    order = jnp.argsort(flat_e).astype(I32)
    experts = jnp.arange(n_experts, dtype=I32)
    counts = jnp.sum(flat_e[:, None] == experts[None, :], axis=0, dtype=I32)
    start = jnp.cumsum(counts) - counts
    blocks_of = (counts + tm - 1) // tm
    blk_end = jnp.cumsum(blocks_of)
    n_blocks = -(-n_assign // tm) + n_experts
    n_used = blk_end[-1]
    blk = jnp.minimum(jnp.arange(n_blocks, dtype=I32), n_used - 1)
    blk_exp = jnp.sum(blk_end[None, :] <= blk[:, None], axis=1, dtype=I32)
    local = blk - (blk_end - blocks_of)[blk_exp]
    blk_base = start[blk_exp] + local * tm
    n_valid = jnp.clip(counts[blk_exp] - local * tm, 0, tm)
    order = jnp.concatenate([order, jnp.zeros((tm,), I32)])
    return blk_exp, n_used.reshape(1), n_valid, blk_base, order


def _combine_rows(y_ref, r_ref, h_ref, g_ref, b_ref, alpha):
    d = h_ref.shape[1]
    r = r_ref[...]
    ffn = y_ref[:, :d] * r[:, 2:3] + y_ref[:, d:] * r[:, 3:4]
    return _layer_norm(alpha * h_ref[...] + ffn, g_ref[...], b_ref[...])


def _combine_kernel(y_ref, r_ref, h_ref, g_ref, b_ref, o_ref, *, alpha):
    o_ref[...] = _combine_rows(y_ref, r_ref, h_ref, g_ref, b_ref, alpha)


def _combine(y_pairs, r_all, h_all, g, b, alpha, first_tile, n_tiles):
    d = h_all.shape[1]
    tm = ROW_TILE
    row = lambda wd: pl.BlockSpec((tm, wd), lambda i: (i + first_tile, 0))
    const = lambda a: pl.BlockSpec(a.shape, lambda i: (0,) * a.ndim)
    return pl.pallas_call(
        functools.partial(_combine_kernel, alpha=alpha),
        grid=(n_tiles,),
        in_specs=[row(2 * d), row(LANES), row(d), const(g), const(b)],
        out_specs=pl.BlockSpec((tm, d), lambda i: (i, 0)),
        out_shape=jax.ShapeDtypeStruct((n_tiles * tm, d), F32),
        compiler_params=_params(("arbitrary",)),
    )(y_pairs, r_all, h_all, g, b)


def _combine_prompt_kernel(y_ref, r_ref, h_ref, g_ref, b_ref, out_hbm, obuf, sem, *, alpha, n_meta,
                           tiles_per_seq):
    i = pl.program_id(0)
    n_steps = pl.num_programs(0)
    tm = obuf.shape[1]
    seq = out_hbm.shape[1]
    last_rows = seq + n_meta - (tiles_per_seq - 1) * tm
    kinds = ("first", "middle", "last")

    def kind_is(j, kind):
        return {"first": j == 0, "last": j == tiles_per_seq - 1,
                "middle": jnp.logical_and(j > 0, j < tiles_per_seq - 1)}[kind]

    def tile_copy(step, kind):
        b, j, s = step // tiles_per_seq, step % tiles_per_seq, step % 2
        if kind == "first":
            src, dst = obuf.at[s, pl.ds(n_meta, tm - n_meta)], out_hbm.at[b, pl.ds(0, tm - n_meta)]
        elif kind == "last":
            src = obuf.at[s, pl.ds(0, last_rows)]
            dst = out_hbm.at[b, pl.ds((tiles_per_seq - 1) * tm - n_meta, last_rows)]
        else:
            src, dst = obuf.at[s], out_hbm.at[b, pl.ds(pl.multiple_of(j * tm - n_meta, 8), tm)]
        return pltpu.make_async_copy(src, dst, sem.at[s])

    def for_kind(step, action):
        for kind in kinds:
            @pl.when(kind_is(step % tiles_per_seq, kind))
            def _():
                action(tile_copy(step, kind))

    @pl.when(i >= 2)
    def _():
        for_kind(i - 2, lambda cp: cp.wait())

    obuf[i % 2] = _combine_rows(y_ref, r_ref, h_ref, g_ref, b_ref, alpha)
    for_kind(i, lambda cp: cp.start())

    @pl.when(i == n_steps - 1)
    def _():
        @pl.when(i >= 1)
        def _():
            for_kind(i - 1, lambda cp: cp.wait())
        for_kind(i, lambda cp: cp.wait())


def _combine_prompt(y_pairs, r_all, h_all, g, b, alpha, bsz, lp, seq, n_meta):
    d = h_all.shape[1]
    tm = PROJ_TILE
    tiles_per_seq = lp // tm
    assert lp % tm == 0 and tiles_per_seq >= 2 and n_meta % 8 == 0 and n_meta < tm
    assert 0 < seq + n_meta - (tiles_per_seq - 1) * tm <= tm and (seq + n_meta) % 8 == 0
    row = lambda wd: pl.BlockSpec((tm, wd), lambda i: (i, 0))
    const = lambda a: pl.BlockSpec(a.shape, lambda i: (0,) * a.ndim)
    return pl.pallas_call(
        functools.partial(_combine_prompt_kernel, alpha=alpha, n_meta=n_meta, tiles_per_seq=tiles_per_seq),
        grid=(bsz * tiles_per_seq,),
        in_specs=[row(2 * d), row(LANES), row(d), const(g), const(b)],
        out_specs=pl.BlockSpec(memory_space=pl.ANY),
        out_shape=jax.ShapeDtypeStruct((bsz, seq, d), F32),
        scratch_shapes=[pltpu.VMEM((2, tm, d), F32), pltpu.SemaphoreType.DMA((2,))],
        compiler_params=_params(("arbitrary",)),
    )(y_pairs, r_all, h_all, g, b)


def kernel(x_prompt, x_sample, cache_k_sb, cache_v_sb, cache_k_diff, cache_v_diff, page_table, meta_tokens, ln_in_g, ln_in_b, rel_bias, w_in, sb_norm_g, diff_lambda, diff_subln_g, w_out, ln1_g, ln1_b, w_group, b_group, w_expert_router, b_expert_router, w_gate_up, w_down, ln2_g, ln2_b):
    bsz, seq, d = x_prompt.shape
    db = x_sample.shape[0]
    assert x_sample.shape[1] == 1
    depth = w_in.shape[0]
    assert depth == 1, "the prompt/decode buffers below are laid out for a single layer"
    n_meta = meta_tokens.shape[0]
    hd = sb_norm_g.shape[-1]
    sb_kv, diff_kv = cache_k_sb.shape[3], cache_k_diff.shape[3]
    sb_w, diff_w = d // 2, d // 2
    sb_kv_w, diff_kv_w = sb_kv * hd, diff_kv * 2 * hd
    splits = [0, sb_w, sb_w + sb_kv_w, sb_w + 2 * sb_kv_w, sb_w + 2 * sb_kv_w + diff_w,
              sb_w + 2 * sb_kv_w + diff_w + diff_kv_w, sb_w + 2 * sb_kv_w + diff_w + 2 * diff_kv_w]
    assert splits[-1] == w_in.shape[-1]
    n_groups = w_group.shape[-1]
    n_experts = w_expert_router.shape[-1]
    per_group = n_experts // n_groups
    page = cache_k_sb.shape[2]
    past = page_table.shape[1] * page
    alpha = (2 * depth) ** 0.25
    li = 0
    lam_init = 0.8 - 0.6 * math.exp(-0.3 * li)

    seq_len = n_meta + seq
    lp = -(-seq_len // (LANES * PROJ_TILE // math.gcd(LANES, PROJ_TILE))) * (LANES * PROJ_TILE // math.gcd(LANES, PROJ_TILE))
    xpad = jnp.concatenate([jnp.broadcast_to(meta_tokens[None], (bsz, n_meta, d)), x_prompt,
                            jnp.zeros((bsz, lp - seq_len, d), F32)], axis=1)
    g_in, b_in = ln_in_g.reshape(1, d), ln_in_b.reshape(1, d)
    w_in16 = w_in[li].astype(BF16)
    w_out16 = w_out[li].astype(BF16)
    wgu16 = w_gate_up[li].astype(BF16)
    w_router = jnp.zeros((d, LANES), F32).at[:, :n_groups].set(w_group[li]) \
        .at[:, n_groups:n_groups + n_experts].set(w_expert_router[li])
    b_router = jnp.zeros((1, LANES), F32).at[0, :n_groups].set(b_group[li]) \
        .at[0, n_groups:n_groups + n_experts].set(b_expert_router[li])
    w_router = w_router.astype(BF16)
    g1, b1 = ln1_g[li].reshape(1, d), ln1_b[li].reshape(1, d)
    g2, b2 = ln2_g[li].reshape(1, d), ln2_b[li].reshape(1, d)
    g_sb = sb_norm_g[li].reshape(1, hd)
    g_diff = diff_subln_g[li].reshape(1, 2 * hd)
    lam_params = diff_lambda[li]
    idx = jnp.arange(MXU_DIM)
    tri_half = (idx[:LANES, None] >= idx[None, :LANES])
    tri = jnp.concatenate([tri_half, jnp.ones((LANES, LANES), bool)], axis=1).astype(BF16)
    tri = jnp.concatenate([tri, tri], axis=0)

    near, far, dec = _bias_tables(rel_bias, past)

    (hp, qsb16, qd16, ksb16, vsb16, kd16, vd16, k_sb, v_sb, k_d, v_d) = _ln_proj_prompt(
        xpad, g_in, b_in, w_in16, splits, seq_len, sb_kv, diff_kv)
    o_sb = _sb_prompt(qsb16, ksb16, vsb16, tri, g_sb, sb_kv)
    o_d = _diff_prompt(qd16, kd16, vd16, near, far, lam_params, g_diff, diff_kv, lam_init)
    n_prompt = bsz * lp
    total_rows = n_prompt + ROW_TILE
    assert n_prompt % ROW_TILE == 0 and db <= ROW_TILE
    h1_all, r_all = _merge(o_sb.reshape(n_prompt, sb_w), o_d.reshape(n_prompt, diff_w),
                           hp.reshape(n_prompt, d), w_out16, g1, b1, w_router, b_router, total_rows, 0,
                           None, alpha, n_groups, per_group)

    hs, proj_s = _ln_proj_rows(x_sample.reshape(db, d), g_in, b_in, w_in16)
    qn_sb, kn_sb, vn_sb, qn_d, kn_d, vn_d = [proj_s[:, splits[t]:splits[t + 1]] for t in range(6)]
    pt_flat = page_table.reshape(-1).astype(I32)
    os_sb = _sb_decode(pt_flat, qn_sb, tri, g_sb, cache_k_sb, cache_v_sb, li)
    dec_heads = dec[:, 0, :].reshape(diff_kv, -1, past)
    dec_rows = jnp.concatenate([dec_heads] * 2, axis=1)
    bias0 = jnp.concatenate([(rel_bias[0] * LOG2E).reshape(diff_kv, -1, 1)] * 2, axis=1)
    os_d = _diff_decode(pt_flat, qn_d, kn_d, vn_d, dec_rows, bias0, lam_params, g_diff, cache_k_diff,
                        cache_v_diff, li, lam_init)
    pad_rows = lambda a: jnp.concatenate([a, jnp.zeros((ROW_TILE - db, a.shape[1]), a.dtype)], axis=0)
    h1_all, r_all = _merge(pad_rows(os_sb), pad_rows(os_d), pad_rows(hs), w_out16, g1, b1, w_router,
                           b_router, total_rows, n_prompt // ROW_TILE, (h1_all, r_all), alpha, n_groups,
                           per_group)

    expert_idx = r_all[:, :TOP_K].astype(I32)
    tables = _dispatch_tables(expert_idx, n_experts, ROW_TILE)
    y_pairs = _moe(*tables, h1_all, wgu16, w_down[li])
    y_prompt = _combine_prompt(y_pairs, r_all, h1_all, g2, b2, alpha, bsz, lp, seq, n_meta)
    y_sample = _combine(y_pairs, r_all, h1_all, g2, b2, alpha, n_prompt // ROW_TILE, 1)[:db].reshape(db, 1, d)
    return (y_prompt, y_sample,
            k_sb[None], v_sb[None], k_d[None], v_d[None],
            kn_sb.reshape(1, db, 1, sb_kv, hd), vn_sb.reshape(1, db, 1, sb_kv, hd),
            kn_d.reshape(1, db, 1, diff_kv, 2 * hd), vn_d.reshape(1, db, 1, diff_kv, 2 * hd))
```
